```python
import math
import jax
import jax.numpy as jnp
from jax import lax
import numpy as np

D_MODEL = 4096
BATCH = 32
SEQ = 256
DEPTH = 2
DEC_BATCH = 8
DEC_SEQ = 1024
PAST_LEN = 512

GRID_W = 64
EPS = 1e-6
D_LRU = D_MODEL // 4
H_LRU = 8
BS_LRU = D_LRU // H_LRU
CONV_W = 4
CONV_PAD = (2, 1)
LRU_C = 8.0
H_DIFF = 8
DH_DIFF = D_MODEL // (4 * H_DIFF)
D_DIFF = H_DIFF * 2 * DH_DIFF
Q_BLOCK = 128
ROPE_BASE = 10000.0
H_GLA = 4
DV_GLA = D_MODEL // (4 * H_GLA)
DK_GLA = DV_GLA // 2
D_GLA_K = H_GLA * DK_GLA
D_GLA_V = H_GLA * DV_GLA
GLA_RANK = 16
GLA_TAU = 16.0
GLA_CHUNK = 64
SPLIT_SIZES = (D_LRU, D_LRU, D_DIFF, D_DIFF, D_DIFF, D_GLA_K, D_GLA_K, D_GLA_V, D_GLA_V, 2 * GLA_RANK)
N_IN = 2 * D_LRU + 3 * D_DIFF + 2 * D_GLA_K + 2 * D_GLA_V + 2 * GLA_RANK
N_EXPERTS = 64
TOP_K = 8
N_GROUPS = 8
TOPK_GROUPS = 4
D_EXPERT = D_MODEL // 4
D_SHARED = D_EXPERT
ROUTED_SCALE = 2.5
MOE_BLOCK = 128

kernel_name = 'hybrid_diffusion_prefix_trunk_step'


def rmsnorm(x, g):
    xf = x.astype(jnp.float32)
    y = xf * lax.rsqrt(jnp.mean(xf * xf, axis=-1, keepdims=True) + EPS)
    return (y * g.astype(jnp.float32)).astype(x.dtype)


def split_points():
    pts, acc = [], 0
    for s in SPLIT_SIZES[:-1]:
        acc += s
        pts.append(acc)
    return pts


def centred_dwconv(x, w, b):
    y = lax.conv_general_dilated(x, w[:, None, :].astype(x.dtype), window_strides=(1,), padding=[CONV_PAD],
                                 dimension_numbers=('NWC', 'WIO', 'NWC'), feature_group_count=x.shape[-1])
    return y + b.astype(x.dtype)


def lin_combine(left, right):
    a1, b1 = left
    a2, b2 = right
    return a1 * a2, a2 * b1 + b2


def rglru(x, wa, ba, wx, bx, lam, h0):
    B, T, _ = x.shape
    xf = x.astype(jnp.float32)
    xb = xf.reshape(B, T, H_LRU, BS_LRU)
    r = jax.nn.sigmoid(jnp.einsum('bthi,hij->bthj', xb, wa).reshape(B, T, D_LRU) + ba)
    i = jax.nn.sigmoid(jnp.einsum('bthi,hij->bthj', xb, wx).reshape(B, T, D_LRU) + bx)
    log_a = -LRU_C * r * jax.nn.softplus(-lam.astype(jnp.float32))
    a = jnp.exp(log_a)
    u = jnp.sqrt(-jnp.expm1(2.0 * log_a)) * (i * xf)
    u = u.at[:, 0].add(a[:, 0] * h0.astype(jnp.float32))
    _, hs = lax.associative_scan(lin_combine, (a, u), axis=1)
    return hs


def axial_rope(n):
    rows = n // GRID_W
    t = jnp.arange(rows * GRID_W)
    row = (t // GRID_W).astype(jnp.float32)
    col = (t % GRID_W).astype(jnp.float32)
    n_freq = DH_DIFF // 4
    inv = ROPE_BASE ** (-jnp.arange(n_freq, dtype=jnp.float32) / n_freq)
    ang = jnp.concatenate([row[:, None] * inv, col[:, None] * inv], axis=-1)
    return jnp.cos(ang), jnp.sin(ang)


def apply_rope(x, cos, sin):
    half = x.shape[-1] // 2
    c = cos[None, :, None, None, :]
    s = sin[None, :, None, None, :]
    xf = x.astype(jnp.float32)
    x1, x2 = xf[..., :half], xf[..., half:]
    return jnp.concatenate([x1 * c - x2 * s, x2 * c + x1 * s], axis=-1).astype(x.dtype)


def diff_attention(q, k, v, lam):
    B, T, H, _, d = q.shape
    nb = T // Q_BLOCK
    qb = q.reshape(B, nb, Q_BLOCK, H, 2, d).swapaxes(0, 1)
    scale = d ** -0.5

    def one_block(qblk):
        s = jnp.einsum('bqhmd,bkhmd->bmhqk', qblk, k).astype(jnp.float32) * scale
        p = jax.nn.softmax(s, axis=-1)
        a = p[:, 0] - lam * p[:, 1]
        return jnp.einsum('bhqk,bkhe->bqhe', a.astype(v.dtype), v)

    o = lax.map(one_block, qb)
    return o.swapaxes(0, 1).reshape(B, T, H, v.shape[-1])


def gla_chunked(q, k, v, glog, s0):
    B, T, H, _ = q.shape
    nc = T // GLA_CHUNK
    dv = v.shape[-1]

    def to_chunks(a):
        return a.astype(jnp.float32).reshape(B, nc, GLA_CHUNK, H, a.shape[-1]).transpose(1, 0, 3, 2, 4)

    mask = jnp.tril(jnp.ones((GLA_CHUNK, GLA_CHUNK), dtype=bool))

    def step(S, inp):
        qc, kc, vc, gc = inp
        b = jnp.cumsum(gc, axis=2)
        qt = qc * jnp.exp(b)
        kt = kc * jnp.exp(-b)
        att = jnp.where(mask, jnp.einsum('bhid,bhjd->bhij', qt, kt), 0.0)
        o = jnp.einsum('bhij,bhjv->bhiv', att, vc) + jnp.einsum('bhid,bhdv->bhiv', qt, S)
        b_end = b[:, :, -1:, :]
        S = jnp.exp(b_end[:, :, 0, :])[..., None] * S + jnp.einsum('bhjd,bhjv->bhdv', kc * jnp.exp(b_end - b), vc)
        return S, o

    S, o = lax.scan(step, s0.astype(jnp.float32), (to_chunks(q), to_chunks(k), to_chunks(v), to_chunks(glog)))
    return o.transpose(1, 0, 3, 2, 4).reshape(B, T, H, dv), S


def token_mixers(h, p, lam_init, ctx, rope):
    B, T, _ = h.shape
    z = h @ p['w_in']
    xa, ga, qb, kb, vb, qc, kc, vc, gc, lr = jnp.split(z, split_points(), axis=-1)

    xa = centred_dwconv(xa, p['conv_w'], p['conv_b'])
    h0 = jnp.zeros((B, 2, D_LRU), jnp.float32) if ctx is None else ctx['lru']
    hf = rglru(xa, p['lru_wa'][0], p['lru_ba'][0], p['lru_wx'][0], p['lru_bx'][0], p['lru_lam'][0], h0[:, 0])
    hb = rglru(xa[:, ::-1], p['lru_wa'][1], p['lru_ba'][1], p['lru_wx'][1], p['lru_bx'][1], p['lru_lam'][1], h0[:, 1])[:, ::-1]
    y_a = ((hf + hb) * jax.nn.gelu(ga.astype(jnp.float32))).astype(h.dtype)

    q = qb.reshape(B, T, H_DIFF, 2, DH_DIFF)
    k = kb.reshape(B, T, H_DIFF, 2, DH_DIFF)
    v = vb.reshape(B, T, H_DIFF, 2 * DH_DIFF)
    if rope is not None:
        q = apply_rope(q, rope[0], rope[1])
        k = apply_rope(k, rope[0], rope[1])
    k_new = k.reshape(B, T, H_DIFF, 2 * DH_DIFF)
    if ctx is None:
        k_all, v_all = k_new, v
    else:
        k_all = jnp.concatenate([ctx['k'].astype(k_new.dtype), k_new], axis=1)
        v_all = jnp.concatenate([ctx['v'].astype(v.dtype), v], axis=1)
    dl = p['diff_lam'].astype(jnp.float32)
    lam = jnp.exp(jnp.sum(dl[0] * dl[1])) - jnp.exp(jnp.sum(dl[2] * dl[3])) + lam_init
    o_b = diff_attention(q, k_all.reshape(B, -1, H_DIFF, 2, DH_DIFF), v_all, lam)
    y_b = (rmsnorm(o_b, p['diff_subln']) * (1.0 - lam_init)).reshape(B, T, D_DIFF).astype(h.dtype)

    qg = qc.reshape(B, T, H_GLA, DK_GLA) * (DK_GLA ** -0.5)
    kg = kc.reshape(B, T, H_GLA, DK_GLA)
    vg = vc.reshape(B, T, H_GLA, DV_GLA)
    lr = lr.reshape(B, T, 2, GLA_RANK).astype(jnp.float32)
    glog = jax.nn.log_sigmoid(jnp.einsum('btdr,drk->btdk', lr, p['gla_w'].astype(jnp.float32))
                              + p['gla_b'].astype(jnp.float32)) / GLA_TAU
    glog = glog.reshape(B, T, 2, H_GLA, DK_GLA)
    s0 = jnp.zeros((B, 2, H_GLA, DK_GLA, DV_GLA), jnp.float32) if ctx is None else ctx['gla']
    of, sf = gla_chunked(qg, kg, vg, glog[:, :, 0], s0[:, 0])
    ob, sb = gla_chunked(qg[:, ::-1], kg[:, ::-1], vg[:, ::-1], glog[:, ::-1, 1], s0[:, 1])
    o_c = of + ob[:, ::-1]
    y_c = (rmsnorm(o_c, p['gla_norm']) * jax.nn.silu(gc.reshape(B, T, H_GLA, DV_GLA).astype(jnp.float32)))
    y_c = y_c.reshape(B, T, D_GLA_V).astype(h.dtype)

    y = jnp.concatenate([y_a, y_b, y_c], axis=-1) @ p['w_out']
    if ctx is None:
        new_ctx = (k_new, v, jnp.stack([hf[:, -1], hb[:, 0]], axis=1), jnp.stack([sf, sb], axis=1))
    else:
        new_ctx = None
    return y, new_ctx


def routed_experts(xf, idx, wts, w_gate, w_up, w_down):
    T, D = xf.shape
    n = T * TOP_K
    n_blocks = -(-(n + N_EXPERTS * (MOE_BLOCK - 1)) // MOE_BLOCK)
    flat_e = idx.reshape(n)
    order = jnp.argsort(flat_e)
    e_sorted = flat_e[order]
    tok_sorted = (order // TOP_K).astype(jnp.int32)
    w_sorted = wts.reshape(n)[order]
    counts = jnp.bincount(flat_e, length=N_EXPERTS)
    padded = (counts + MOE_BLOCK - 1) // MOE_BLOCK * MOE_BLOCK
    start = jnp.cumsum(counts) - counts
    pend = jnp.cumsum(padded)
    pstart = pend - padded
    dest = pstart[e_sorted] + jnp.arange(n) - start[e_sorted]
    slot_tok = jnp.full((n_blocks * MOE_BLOCK,), T, jnp.int32).at[dest].set(tok_sorted)
    slot_w = jnp.zeros((n_blocks * MOE_BLOCK,), xf.dtype).at[dest].set(w_sorted)
    blk_e = jnp.minimum(jnp.searchsorted(pend, jnp.arange(n_blocks) * MOE_BLOCK, side='right'), N_EXPERTS - 1)
    xpad = jnp.concatenate([xf, jnp.zeros((1, D), xf.dtype)], axis=0)

    def body(acc, inp):
        tok, wgt, e = inp
        xb = xpad[tok]
        hdn = jax.nn.silu(xb @ w_gate[e]) * (xb @ w_up[e])
        return acc.at[tok].add((hdn @ w_down[e]) * wgt[:, None]), None

    acc, _ = lax.scan(body, jnp.zeros((T + 1, D), xf.dtype),
                      (slot_tok.reshape(n_blocks, MOE_BLOCK), slot_w.reshape(n_blocks, MOE_BLOCK), blk_e))
    return acc[:T]


def moe_ffn(h, p):
    B, T, D = h.shape
    xf = h.reshape(B * T, D)
    s = jax.nn.sigmoid((xf @ p['router_w']).astype(jnp.float32))
    sel = s + p['router_b'].astype(jnp.float32)
    per_group = N_EXPERTS // N_GROUPS
    gscore = lax.top_k(sel.reshape(-1, N_GROUPS, per_group), 2)[0].sum(-1)
    _, gidx = lax.top_k(gscore, TOPK_GROUPS)
    gmask = jax.nn.one_hot(gidx, N_GROUPS, dtype=jnp.float32).sum(-2)
    sel = jnp.where(jnp.repeat(gmask, per_group, axis=-1) > 0, sel, -jnp.inf)
    _, idx = lax.top_k(sel, TOP_K)
    w = jnp.take_along_axis(s, idx, axis=-1)
    w = w / jnp.sum(w, axis=-1, keepdims=True) * ROUTED_SCALE
    routed = routed_experts(xf, idx, w.astype(xf.dtype), p['exp_w_gate'], p['exp_w_up'], p['exp_w_down'])
    shared = (jax.nn.silu(xf @ p['sh_w_gate']) * (xf @ p['sh_w_up'])) @ p['sh_w_down']
    return (routed + shared).reshape(B, T, D)


def trunk_layer(x, mod, p, lam_init, ctx, rope):
    sh1, sc1, g1, sh2, sc2, g2 = jnp.split(mod, 6, axis=-1)
    g = p['norm_g']
    h = rmsnorm(x, g[0]) * (1.0 + sc1) + sh1
    y, new_ctx = token_mixers(h, p, lam_init, ctx, rope)
    x = x + g1 * rmsnorm(y, g[1])
    h = rmsnorm(x, g[2]) * (1.0 + sc2) + sh2
    x = x + g2 * rmsnorm(moe_ffn(h, p), g[3])
    return x, new_ctx


def setup_inputs(seed: int = 0) -> dict:
    key = jax.random.key(seed)
    k = jax.random.split(key, 36)
    f32 = jnp.float32
    L = DEPTH

    def nrm(kk, shape, scale):
        return jax.random.normal(kk, shape, f32) * scale

    u = jax.random.uniform(k[14], (L, 2, D_LRU), f32, 0.9, 0.999)
    s = u ** (1.0 / LRU_C)
    return {
        'x_prompt': nrm(k[0], (BATCH, SEQ, D_MODEL), 1.0),
        'x_sample': nrm(k[1], (DEC_BATCH, DEC_SEQ, D_MODEL), 1.0),
        'cache_k': nrm(k[2], (DEC_BATCH, L, PAST_LEN, H_DIFF, 2 * DH_DIFF), 1.0),
        'cache_v': nrm(k[3], (DEC_BATCH, L, PAST_LEN, H_DIFF, 2 * DH_DIFF), 1.0),
        'state_lru': nrm(k[4], (DEC_BATCH, L, 2, D_LRU), 0.5),
        'state_gla': nrm(k[5], (DEC_BATCH, L, 2, H_GLA, DK_GLA, DV_GLA), 0.5),
        'c': nrm(k[6], (DEC_BATCH, D_MODEL), 1.0),
        'c_ctx': nrm(k[7], (D_MODEL,), 1.0),
        'mod_w': nrm(k[8], (L, D_MODEL, 6 * D_MODEL), 0.5 * D_MODEL ** -0.5),
        'mod_b': nrm(k[9], (L, 6 * D_MODEL), 0.02),
        'norm_g': 1.0 + nrm(k[10], (L, 4, D_MODEL), 0.05),
        'w_in': nrm(k[11], (L, D_MODEL, N_IN), D_MODEL ** -0.5),
        'conv_w': nrm(k[12], (L, CONV_W, D_LRU), CONV_W ** -0.5),
        'conv_b': nrm(k[13], (L, D_LRU), 0.02),
        'lru_wa': nrm(k[15], (L, 2, H_LRU, BS_LRU, BS_LRU), BS_LRU ** -0.5),
        'lru_ba': nrm(k[16], (L, 2, D_LRU), 0.02),
        'lru_wx': nrm(k[17], (L, 2, H_LRU, BS_LRU, BS_LRU), BS_LRU ** -0.5),
        'lru_bx': nrm(k[18], (L, 2, D_LRU), 0.02),
        'lru_lam': jnp.log(s) - jnp.log1p(-s),
        'diff_lam': nrm(k[19], (L, 4, DH_DIFF), 0.1),
        'diff_subln': 1.0 + nrm(k[20], (L, 2 * DH_DIFF), 0.05),
        'gla_w': nrm(k[21], (L, 2, GLA_RANK, D_GLA_K), GLA_RANK ** -0.5),
        'gla_b': nrm(k[22], (L, 2, D_GLA_K), 0.02),
        'gla_norm': 1.0 + nrm(k[23], (L, DV_GLA), 0.05),
        'w_out': nrm(k[24], (L, D_MODEL, D_MODEL), D_MODEL ** -0.5),
        'router_w': nrm(k[25], (L, D_MODEL, N_EXPERTS), D_MODEL ** -0.5),
        'router_b': nrm(k[26], (L, N_EXPERTS), 0.01),
        'exp_w_gate': nrm(k[27], (L, N_EXPERTS, D_MODEL, D_EXPERT), D_MODEL ** -0.5),
        'exp_w_up': nrm(k[28], (L, N_EXPERTS, D_MODEL, D_EXPERT), D_MODEL ** -0.5),
        'exp_w_down': nrm(k[29], (L, N_EXPERTS, D_EXPERT, D_MODEL), D_EXPERT ** -0.5),
        'sh_w_gate': nrm(k[30], (L, D_MODEL, D_SHARED), D_MODEL ** -0.5),
        'sh_w_up': nrm(k[31], (L, D_MODEL, D_SHARED), D_MODEL ** -0.5),
        'sh_w_down': nrm(k[32], (L, D_SHARED, D_MODEL), D_SHARED ** -0.5),
    }


def reference(x_prompt, x_sample, cache_k, cache_v, state_lru, state_gla, c, c_ctx,
              mod_w, mod_b, norm_g, w_in, conv_w, conv_b, lru_wa, lru_ba, lru_wx, lru_bx, lru_lam,
              diff_lam, diff_subln, gla_w, gla_b, gla_norm, w_out, router_w, router_b,
              exp_w_gate, exp_w_up, exp_w_down, sh_w_gate, sh_w_up, sh_w_down):
    rope = axial_rope(x_sample.shape[1])
    yp, ys = x_prompt, x_sample
    ks, vs, lrus, glas = [], [], [], []
    for l in range(DEPTH):
        p = {'w_in': w_in[l], 'conv_w': conv_w[l], 'conv_b': conv_b[l],
             'lru_wa': lru_wa[l], 'lru_ba': lru_ba[l], 'lru_wx': lru_wx[l], 'lru_bx': lru_bx[l], 'lru_lam': lru_lam[l],
             'diff_lam': diff_lam[l], 'diff_subln': diff_subln[l],
             'gla_w': gla_w[l], 'gla_b': gla_b[l], 'gla_norm': gla_norm[l],
             'w_out': w_out[l], 'norm_g': norm_g[l], 'router_w': router_w[l], 'router_b': router_b[l],
             'exp_w_gate': exp_w_gate[l], 'exp_w_up': exp_w_up[l], 'exp_w_down': exp_w_down[l],
             'sh_w_gate': sh_w_gate[l], 'sh_w_up': sh_w_up[l], 'sh_w_down': sh_w_down[l]}
        lam_init = 0.8 - 0.6 * math.exp(-0.3 * l)
        mod_ctx = (jax.nn.silu(c_ctx) @ mod_w[l] + mod_b[l])[None, None, :]
        mod_lat = (jax.nn.silu(c) @ mod_w[l] + mod_b[l])[:, None, :]
        yp, (k_l, v_l, lru_l, gla_l) = trunk_layer(yp, mod_ctx, p, lam_init, None, None)
        ks.append(k_l)
        vs.append(v_l)
        lrus.append(lru_l)
        glas.append(gla_l)
        ctx = {'k': cache_k[:, l], 'v': cache_v[:, l], 'lru': state_lru[:, l], 'gla': state_gla[:, l]}
        ys, _ = trunk_layer(ys, mod_lat, p, lam_init, ctx, rope)
    return (yp, ys, jnp.stack(ks, axis=1), jnp.stack(vs, axis=1), jnp.stack(lrus, axis=1), jnp.stack(glas, axis=1))
```

```python
import functools
import math

import jax
import jax.numpy as jnp
from jax import lax
from jax.experimental import pallas as pl
from jax.experimental.pallas import tpu as pltpu

F32 = jnp.float32
BF16 = jnp.bfloat16
U32 = jnp.uint32
I32 = jnp.int32

EPS = 1e-6
GRID_W = 64
H_LRU = 8
CONV_W = 4
CONV_LEFT = 2
LRU_C = 8.0
H_DIFF = 8
ROPE_BASE = 10000.0
H_GLA = 4
GLA_RANK = 16
GLA_TAU = 16.0
GLA_CHUNK = 64
N_EXPERTS = 64
TOP_K = 8
N_GROUPS = 8
TOPK_GROUPS = 4
ROUTED_SCALE = 2.5

V7X_VMEM_BYTES = 64 * 1024 * 1024
LANES = 128
SUBLANES = 8
MIB = 1024 * 1024


def _cparams(semantics, vmem_mib):
    assert vmem_mib * MIB < V7X_VMEM_BYTES
    return pltpu.CompilerParams(dimension_semantics=semantics, vmem_limit_bytes=vmem_mib * MIB)


def _rms(x):
    return x * lax.rsqrt(jnp.mean(x * x, axis=-1, keepdims=True) + EPS)


def _sigmoid(x):
    return 1.0 / (1.0 + jnp.exp(-x))


def _silu(x):
    return x * _sigmoid(x)


def _dot(a, b):
    return jnp.dot(a, b, preferred_element_type=F32)


def _dot_nt(a, b):
    return lax.dot_general(a, b, (((1,), (1,)), ((), ())), preferred_element_type=F32)


def _dot_tn(a, b):
    return lax.dot_general(a, b, (((0,), (0,)), ((), ())), preferred_element_type=F32)


def _mod_kernel(c_ref, w_ref, b_ref, o_ref):
    s = _silu(c_ref[...]).astype(BF16)
    o_ref[...] = _dot(s, w_ref[...].astype(BF16)) + b_ref[...]


def modulation(c_rows, mod_w, mod_b):
    L, D, N6 = mod_w.shape
    R = c_rows.shape[0]
    tn = 512
    return pl.pallas_call(
        _mod_kernel,
        grid=(L, N6 // tn),
        in_specs=[
            pl.BlockSpec((R, D), lambda l, j: (0, 0)),
            pl.BlockSpec((None, D, tn), lambda l, j: (l, 0, j)),
            pl.BlockSpec((None, 1, tn), lambda l, j: (l, 0, j)),
        ],
        out_specs=pl.BlockSpec((None, R, tn), lambda l, j: (l, 0, j)),
        out_shape=jax.ShapeDtypeStruct((L, R, N6), F32),
        compiler_params=_cparams(("arbitrary", "arbitrary"), 40),
        name="modulation",
    )(c_rows, mod_w, mod_b.reshape(L, 1, N6))


def _mod_row_index(i, tm, n_ctx, t_lat):
    r0 = i * tm
    return jnp.where(r0 < n_ctx, 0, 1 + (r0 - n_ctx) // t_lat)


def _norm_mod_kernel(x_ref, g_ref, mod_ref, o_ref):
    y = _rms(x_ref[...]) * g_ref[...]
    o_ref[...] = (y * (1.0 + mod_ref[1:2, :]) + mod_ref[0:1, :]).astype(o_ref.dtype)


def norm_mod(x, g, mods, n_ctx, t_lat):
    N, D = x.shape
    tm = 256
    return pl.pallas_call(
        _norm_mod_kernel,
        grid=(N // tm,),
        in_specs=[
            pl.BlockSpec((tm, D), lambda i: (i, 0)),
            pl.BlockSpec((1, D), lambda i: (0, 0)),
            pl.BlockSpec((None, 6, D), lambda i: (_mod_row_index(i, tm, n_ctx, t_lat), 0, 0)),
        ],
        out_specs=pl.BlockSpec((tm, D), lambda i: (i, 0)),
        out_shape=jax.ShapeDtypeStruct((N, D), BF16),
        compiler_params=_cparams(("arbitrary",), 32),
        name="norm_mod",
    )(x, g, mods)


def _mm_kernel(x_ref, w_ref, o_ref):
    o_ref[...] = _dot(x_ref[...], w_ref[...]).astype(o_ref.dtype)


def matmul(x, w, out_dtype=F32, tm=1024, tn=512):
    M, K = x.shape
    N = w.shape[1]
    tm = min(tm, M)
    tn = min(tn, N)
    assert M % tm == 0 and N % tn == 0
    return pl.pallas_call(
        _mm_kernel,
        grid=(M // tm, N // tn),
        in_specs=[pl.BlockSpec((tm, K), lambda i, j: (i, 0)), pl.BlockSpec((K, tn), lambda i, j: (0, j))],
        out_specs=pl.BlockSpec((tm, tn), lambda i, j: (i, j)),
        out_shape=jax.ShapeDtypeStruct((M, N), out_dtype),
        compiler_params=_cparams(("arbitrary", "arbitrary"), 48),
        name="matmul",
    )(x, w)


def _swiglu_kernel(x_ref, wg_ref, wu_ref, o_ref):
    x = x_ref[...]
    o_ref[...] = (_silu(_dot(x, wg_ref[...])) * _dot(x, wu_ref[...])).astype(o_ref.dtype)


def swiglu(x, wg, wu, tm=1024, tn=256):
    M, K = x.shape
    N = wg.shape[1]
    tm = min(tm, M)
    return pl.pallas_call(
        _swiglu_kernel,
        grid=(M // tm, N // tn),
        in_specs=[
            pl.BlockSpec((tm, K), lambda i, j: (i, 0)),
            pl.BlockSpec((K, tn), lambda i, j: (0, j)),
            pl.BlockSpec((K, tn), lambda i, j: (0, j)),
        ],
        out_specs=pl.BlockSpec((tm, tn), lambda i, j: (i, j)),
        out_shape=jax.ShapeDtypeStruct((M, N), BF16),
        compiler_params=_cparams(("arbitrary", "arbitrary"), 48),
        name="swiglu",
    )(x, wg, wu)


def _lru_kernel(xa_ref, ga_ref, cw_ref, cb_ref, wg_ref, bg_ref, lam_ref, h0_ref, y_ref, hl_ref,
                pad_ref, af_ref, uf_ref, ab_ref, ub_ref, *, T, CW):
    pad_ref[0:SUBLANES, :] = jnp.zeros((SUBLANES, CW), F32)
    pad_ref[SUBLANES + T:2 * SUBLANES + T, :] = jnp.zeros((SUBLANES, CW), F32)
    pad_ref[SUBLANES:SUBLANES + T, :] = xa_ref[...]
    xc = cb_ref[...] + jnp.zeros((T, CW), F32)
    for j in range(CONV_W):
        o = SUBLANES - CONV_LEFT + j
        xc = xc + cw_ref[j:j + 1, :] * pad_ref[o:o + T, :]
    lam = lam_ref[...]
    nl = -lam
    softplus = jnp.maximum(nl, 0.0) + jnp.log(1.0 + jnp.exp(-jnp.abs(nl)))
    bs = LANES
    for hh in range(CW // bs):
        cs = slice(hh * bs, (hh + 1) * bs)
        xb = xc[:, cs]
        gts = _dot(xb.astype(BF16), wg_ref[hh]) + bg_ref[hh]
        for d, (a_ref, u_ref) in enumerate(((af_ref, uf_ref), (ab_ref, ub_ref))):
            r = _sigmoid(gts[:, (2 * d) * bs:(2 * d + 1) * bs])
            ig = _sigmoid(gts[:, (2 * d + 1) * bs:(2 * d + 2) * bs])
            log_a = (-LRU_C) * r * softplus[d:d + 1, cs]
            a = jnp.exp(log_a)
            a_ref[:, cs] = a
            u_ref[:, cs] = jnp.sqrt(1.0 - a * a) * (ig * xb)

    def step(t, carry):
        hf, hb = carry
        hf = af_ref[pl.ds(t, 1), :] * hf + uf_ref[pl.ds(t, 1), :]
        uf_ref[pl.ds(t, 1), :] = hf
        tb = T - 1 - t
        hb = ab_ref[pl.ds(tb, 1), :] * hb + ub_ref[pl.ds(tb, 1), :]
        ub_ref[pl.ds(tb, 1), :] = hb
        return hf, hb

    hf, hb = lax.fori_loop(0, T, step, (h0_ref[0:1, :], h0_ref[1:2, :]), unroll=8)
    hl_ref[0:1, :] = hf
    hl_ref[1:2, :] = hb
    y_ref[...] = ((uf_ref[...] + ub_ref[...]) * jax.nn.gelu(ga_ref[...], approximate=True)).astype(y_ref.dtype)


def lru_mixer(z, row0, B, T, conv_w, conv_b, w_gate, b_gate, lam, h0):
    d_lru = conv_w.shape[1]
    CW = 512
    nj = d_lru // CW
    rb0 = row0 // T
    kern = functools.partial(_lru_kernel, T=T, CW=CW)
    return pl.pallas_call(
        kern,
        grid=(B, nj),
        in_specs=[
            pl.BlockSpec((T, CW), lambda b, j: (rb0 + b, j)),
            pl.BlockSpec((T, CW), lambda b, j: (rb0 + b, nj + j)),
            pl.BlockSpec((CONV_W, CW), lambda b, j: (0, j)),
            pl.BlockSpec((1, CW), lambda b, j: (0, j)),
            pl.BlockSpec((CW // LANES, LANES, 4 * LANES), lambda b, j: (j, 0, 0)),
            pl.BlockSpec((CW // LANES, 1, 4 * LANES), lambda b, j: (j, 0, 0)),
            pl.BlockSpec((2, CW), lambda b, j: (0, j)),
            pl.BlockSpec((None, 2, CW), lambda b, j: (b, 0, j)),
        ],
        out_specs=[
            pl.BlockSpec((T, CW), lambda b, j: (b, j)),
            pl.BlockSpec((None, 2, CW), lambda b, j: (b, 0, j)),
        ],
        out_shape=[jax.ShapeDtypeStruct((B * T, d_lru), BF16), jax.ShapeDtypeStruct((B, 2, d_lru), F32)],
        scratch_shapes=[pltpu.VMEM((T + 2 * SUBLANES, CW), F32)] + [pltpu.VMEM((T, CW), F32)] * 4,
        compiler_params=_cparams(("arbitrary", "arbitrary"), 40),
        name="lru_mixer",
    )(z, z, conv_w, conv_b, w_gate, b_gate, lam, h0)


def _rope(x, cos2, sin2):
    return x * cos2 + pltpu.roll(x, LANES // 2, axis=1) * sin2


def _attn_kernel(*refs, T, P, tq, use_rope, lam_init):
    it = iter(refs)
    q_ref, k_ref, v_ref = next(it), next(it), next(it)
    ck_ref = cv_ref = cos_ref = sin_ref = None
    if P:
        ck_ref, cv_ref = next(it), next(it)
    if use_rope:
        cos_ref, sin_ref = next(it), next(it)
    dl_ref, sub_ref, o_ref, kall_ref, vall_ref = next(it), next(it), next(it), next(it), next(it)
    dh = LANES
    qi = pl.program_id(2)

    @pl.when(qi == 0)
    def _():
        if P:
            kall_ref[0:P, :] = ck_ref[...].astype(BF16)
            vall_ref[0:P, :] = cv_ref[...].astype(BF16)
        for m in range(2):
            km = k_ref[:, m * dh:(m + 1) * dh]
            if use_rope:
                km = _rope(km, cos_ref[...], sin_ref[...])
            kall_ref[P:P + T, m * dh:(m + 1) * dh] = km.astype(BF16)
        vall_ref[P:P + T, :] = v_ref[...].astype(BF16)

    dl = dl_ref[...]
    lam = (jnp.exp(jnp.sum(dl[0:1, :] * dl[1:2, :], axis=-1, keepdims=True))
           - jnp.exp(jnp.sum(dl[2:3, :] * dl[3:4, :], axis=-1, keepdims=True)) + lam_init)
    scale = dh ** -0.5
    es, invs = [], []
    for m in range(2):
        qm = q_ref[:, m * dh:(m + 1) * dh]
        if use_rope:
            r0 = pl.multiple_of(qi * tq, tq)
            qm = _rope(qm, cos_ref[pl.ds(r0, tq), :], sin_ref[pl.ds(r0, tq), :])
        s = _dot_nt(qm.astype(BF16), kall_ref[:, m * dh:(m + 1) * dh]) * scale
        e = jnp.exp(s - jnp.max(s, axis=-1, keepdims=True))
        es.append(e)
        invs.append(1.0 / jnp.sum(e, axis=-1, keepdims=True))
    a = es[0] * invs[0] - es[1] * (lam * invs[1])
    o = _dot(a.astype(BF16), vall_ref[...])
    o_ref[...] = (_rms(o) * sub_ref[...] * (1.0 - lam_init)).astype(o_ref.dtype)


def diff_attention(z, row0, B, T, col_q, cache_k, cache_v, layer, rope, diff_lam, subln, lam_init):
    H = H_DIFF
    dw = 2 * LANES
    P = 0 if cache_k is None else cache_k.shape[2]
    tq = min(T, 256)
    nq = T // tq
    rbq = row0 // tq
    rbk = row0 // T
    cq = col_q // dw
    kern = functools.partial(_attn_kernel, T=T, P=P, tq=tq, use_rope=rope is not None, lam_init=lam_init)
    in_specs = [
        pl.BlockSpec((tq, dw), lambda b, h, i: (rbq + b * nq + i, cq + h)),
        pl.BlockSpec((T, dw), lambda b, h, i: (rbk + b, cq + H + h)),
        pl.BlockSpec((T, dw), lambda b, h, i: (rbk + b, cq + 2 * H + h)),
    ]
    args = [z, z, z]
    if P:
        in_specs += [pl.BlockSpec((None, None, P, dw), lambda b, h, i: (b, layer, 0, h))] * 2
        args += [cache_k, cache_v]
    if rope is not None:
        in_specs += [pl.BlockSpec((T, LANES), lambda b, h, i: (0, 0))] * 2
        args += [rope[0], rope[1]]
    in_specs += [pl.BlockSpec((4, LANES), lambda b, h, i: (0, 0)), pl.BlockSpec((1, dw), lambda b, h, i: (0, 0))]
    args += [diff_lam, subln]
    return pl.pallas_call(
        kern,
        grid=(B, H, nq),
        in_specs=in_specs,
        out_specs=pl.BlockSpec((tq, dw), lambda b, h, i: (b * nq + i, h)),
        out_shape=jax.ShapeDtypeStruct((B * T, H * dw), BF16),
        scratch_shapes=[pltpu.VMEM((P + T, dw), BF16), pltpu.VMEM((P + T, dw), BF16)],
        compiler_params=_cparams(("arbitrary", "arbitrary", "arbitrary"), 40),
        name="diff_attention",
    )(*args)


def _gla_kernel(q_ref, k_ref, v_ref, gc_ref, lr_ref, wgl_ref, bgl_ref, gn_ref, s0_ref, y_ref, so_ref,
                glog_ref, s_ref, o_ref, *, T):
    C = GLA_CHUNK
    dk = LANES
    nc = T // C
    gl = _dot(lr_ref[...].astype(BF16), wgl_ref[...]) + bgl_ref[...]
    glog_ref[...] = (jnp.minimum(gl, 0.0) - jnp.log(1.0 + jnp.exp(-jnp.abs(gl)))) * (1.0 / GLA_TAU)
    row = lax.broadcasted_iota(I32, (C, C), 0)
    col = lax.broadcasted_iota(I32, (C, C), 1)
    qscale = dk ** -0.5

    for d in range(2):
        keep = (row >= col) if d == 0 else (row <= col)
        tri = jnp.where(keep, 1.0, 0.0).astype(BF16)
        s_ref[...] = s0_ref[d]

        def chunk(ci, carry, d=d, keep=keep, tri=tri):
            c = ci if d == 0 else nc - 1 - ci
            r0 = pl.multiple_of(c * C, C)
            g = glog_ref[pl.ds(r0, C), d * dk:(d + 1) * dk]
            g_hi = g.astype(BF16)
            g_lo = (g - g_hi.astype(F32)).astype(BF16)
            b = _dot(tri, g_hi) + _dot(tri, g_lo)
            b_end = b[C - 1:C, :] if d == 0 else b[0:1, :]
            q = q_ref[pl.ds(r0, C), :] * qscale
            k = k_ref[pl.ds(r0, C), :]
            v = v_ref[pl.ds(r0, C), :].astype(BF16)
            qt = (q * jnp.exp(b)).astype(BF16)
            kt = (k * jnp.exp(-b)).astype(BF16)
            att = jnp.where(keep, _dot_nt(qt, kt), 0.0)
            S = s_ref[...]
            o = _dot(att.astype(BF16), v) + _dot(qt, S.astype(BF16))
            if d == 0:
                o_ref[pl.ds(r0, C), :] = o
            else:
                o_ref[pl.ds(r0, C), :] = o_ref[pl.ds(r0, C), :] + o
            kd = (k * jnp.exp(b_end - b)).astype(BF16)
            ones = jnp.ones((C, dk), BF16)
            bcol = _dot_tn(g_hi, ones) + _dot_tn(g_lo, ones)
            ecol = jnp.exp(bcol)
            s_ref[...] = jnp.concatenate([ecol, ecol], axis=1) * S + _dot_tn(kd, v)
            return carry

        lax.fori_loop(0, nc, chunk, 0)
        so_ref[d] = s_ref[...]

    oc = o_ref[...]
    y_ref[...] = (_rms(oc) * gn_ref[...] * _silu(gc_ref[...])).astype(y_ref.dtype)


def gla_mixer(z, z_lr, row0, B, T, col_q, col_k, col_v, col_g, w_glog, b_glog, gnorm, s0):
    H = H_GLA
    dk, dv = LANES, 2 * LANES
    rb = row0 // T
    kern = functools.partial(_gla_kernel, T=T)
    return pl.pallas_call(
        kern,
        grid=(B, H),
        in_specs=[
            pl.BlockSpec((T, dk), lambda b, h: (rb + b, col_q // dk + h)),
            pl.BlockSpec((T, dk), lambda b, h: (rb + b, col_k // dk + h)),
            pl.BlockSpec((T, dv), lambda b, h: (rb + b, col_v // dv + h)),
            pl.BlockSpec((T, dv), lambda b, h: (rb + b, col_g // dv + h)),
            pl.BlockSpec((T, LANES), lambda b, h: (rb + b, 0)),
            pl.BlockSpec((None, LANES, 2 * dk), lambda b, h: (h, 0, 0)),
            pl.BlockSpec((None, 1, 2 * dk), lambda b, h: (h, 0, 0)),
            pl.BlockSpec((1, dv), lambda b, h: (0, 0)),
            pl.BlockSpec((None, 2, None, dk, dv), lambda b, h: (b, 0, h, 0, 0)),
        ],
        out_specs=[
            pl.BlockSpec((T, dv), lambda b, h: (b, h)),
            pl.BlockSpec((None, 2, None, dk, dv), lambda b, h: (b, 0, h, 0, 0)),
        ],
        out_shape=[jax.ShapeDtypeStruct((B * T, H * dv), BF16), jax.ShapeDtypeStruct((B, 2, H, dk, dv), F32)],
        scratch_shapes=[pltpu.VMEM((T, 2 * dk), F32), pltpu.VMEM((dk, dv), F32), pltpu.VMEM((T, dv), F32)],
        compiler_params=_cparams(("arbitrary", "arbitrary"), 32),
        name="gla_mixer",
    )(z, z, z, z, z_lr, w_glog, b_glog, gnorm, s0)


def _post_mix_kernel(x_ref, y_ref, g_ref, mod_ref, x1_ref, h2_ref, hp_ref):
    D = x_ref.shape[1]
    x1 = x_ref[...] + mod_ref[2:3, :] * (_rms(y_ref[...]) * g_ref[1:2, :])
    x1_ref[...] = x1
    h = (_rms(x1) * g_ref[2:3, :]) * (1.0 + mod_ref[4:5, :]) + mod_ref[3:4, :]
    hb = h.astype(BF16)
    h2_ref[...] = hb
    u = pltpu.bitcast(hb.astype(F32), U32)
    hp_ref[...] = (u[:, :D // 2] >> 16) | (u[:, D // 2:] & jnp.uint32(0xFFFF0000))


def post_mix(x, y, g, mods, n_ctx, t_lat):
    N, D = x.shape
    tm = 256
    return pl.pallas_call(
        _post_mix_kernel,
        grid=(N // tm,),
        in_specs=[
            pl.BlockSpec((tm, D), lambda i: (i, 0)),
            pl.BlockSpec((tm, D), lambda i: (i, 0)),
            pl.BlockSpec((4, D), lambda i: (0, 0)),
            pl.BlockSpec((None, 6, D), lambda i: (_mod_row_index(i, tm, n_ctx, t_lat), 0, 0)),
        ],
        out_specs=[
            pl.BlockSpec((tm, D), lambda i: (i, 0)),
            pl.BlockSpec((tm, D), lambda i: (i, 0)),
            pl.BlockSpec((tm, D // 2), lambda i: (i, 0)),
        ],
        out_shape=[jax.ShapeDtypeStruct((N, D), F32), jax.ShapeDtypeStruct((N, D), BF16),
                   jax.ShapeDtypeStruct((N, D // 2), U32)],
        compiler_params=_cparams(("arbitrary",), 48),
        name="post_mix",
    )(x, y, g, mods)


def _router_kernel(h_ref, rwt_ref, rb_ref, upper_ref, idx_ref, w_ref, rank_ref, cnt_ref, carry_ref, *, tm):
    i = pl.program_id(0)
    per = N_EXPERTS // N_GROUPS

    @pl.when(i == 0)
    def _():
        carry_ref[...] = jnp.zeros_like(carry_ref)

    logits = _dot_nt(rwt_ref[...], h_ref[...])
    s_all = _sigmoid(logits)
    sel_all = s_all + rb_ref[...][:, 0:1]
    sub = lax.broadcasted_iota(I32, (per, tm), 0)
    neg = -jnp.inf
    s_g, sel_g, gscore = [], [], []
    for g in range(N_GROUPS):
        blk = sel_all[g * per:(g + 1) * per, :]
        s_g.append(s_all[g * per:(g + 1) * per, :])
        sel_g.append(blk)
        m1 = jnp.max(blk, axis=0, keepdims=True)
        first = jnp.min(jnp.where(blk == m1, sub, per), axis=0, keepdims=True)
        m2 = jnp.max(jnp.where(sub == first, neg, blk), axis=0, keepdims=True)
        gscore.append(m1 + m2)
    for g in range(N_GROUPS):
        beat = jnp.zeros((1, tm), F32)
        for g2 in range(N_GROUPS):
            if g2 == g:
                continue
            wins = (gscore[g2] > gscore[g]) | ((gscore[g2] == gscore[g]) & (g2 < g))
            beat = beat + jnp.where(wins, 1.0, 0.0)
        sel_g[g] = jnp.where(beat < TOPK_GROUPS, sel_g[g], neg)
    eidx = [sub + g * per for g in range(N_GROUPS)]
    onehot = [jnp.zeros((per, tm), F32) for _ in range(N_GROUPS)]
    picks, wts = [], []
    for k in range(TOP_K):
        m = sel_g[0]
        for g in range(1, N_GROUPS):
            m = jnp.maximum(m, sel_g[g])
        m = jnp.max(m, axis=0, keepdims=True)
        cand = jnp.where(sel_g[0] == m, eidx[0], N_EXPERTS)
        for g in range(1, N_GROUPS):
            cand = jnp.minimum(cand, jnp.where(sel_g[g] == m, eidx[g], N_EXPERTS))
        pick = jnp.min(cand, axis=0, keepdims=True)
        wk = jnp.zeros((per, tm), F32)
        for g in range(N_GROUPS):
            hit = eidx[g] == pick
            wk = wk + jnp.where(hit, s_g[g], 0.0)
            onehot[g] = onehot[g] + jnp.where(hit, 1.0, 0.0)
            sel_g[g] = jnp.where(hit, neg, sel_g[g])
        picks.append(pick)
        wts.append(jnp.sum(wk, axis=0, keepdims=True))
    wsum = wts[0]
    for k in range(1, TOP_K):
        wsum = wsum + wts[k]
    oh = jnp.concatenate(onehot, axis=0)
    carry = carry_ref[...][:, 0:1]
    before = _dot(oh.astype(BF16), upper_ref[...]) + carry
    for k in range(TOP_K):
        idx_ref[k:k + 1, :] = picks[k]
        w_ref[k:k + 1, :] = wts[k] / wsum * ROUTED_SCALE
        rk = jnp.zeros((per, tm), F32)
        for g in range(N_GROUPS):
            rk = rk + jnp.where(eidx[g] == picks[k], before[g * per:(g + 1) * per, :], 0.0)
        rank_ref[k:k + 1, :] = jnp.sum(rk, axis=0, keepdims=True).astype(I32)
    carry_ref[...] = carry_ref[...] + jnp.sum(oh, axis=1, keepdims=True)
    cnt_ref[...] = carry_ref[...]


def router(h2, router_wt, router_b):
    N, D = h2.shape
    tm = 512 if N % 512 == 0 else 256
    E = N_EXPERTS
    upper = (jnp.arange(tm)[:, None] < jnp.arange(tm)[None, :]).astype(BF16)
    kern = functools.partial(_router_kernel, tm=tm)
    return pl.pallas_call(
        kern,
        grid=(N // tm,),
        in_specs=[
            pl.BlockSpec((tm, D), lambda i: (i, 0)),
            pl.BlockSpec((E, D), lambda i: (0, 0)),
            pl.BlockSpec((E, LANES), lambda i: (0, 0)),
            pl.BlockSpec((tm, tm), lambda i: (0, 0)),
        ],
        out_specs=[
            pl.BlockSpec((TOP_K, tm), lambda i: (0, i)),
            pl.BlockSpec((TOP_K, tm), lambda i: (0, i)),
            pl.BlockSpec((TOP_K, tm), lambda i: (0, i)),
            pl.BlockSpec((E, LANES), lambda i: (0, 0)),
        ],
        out_shape=[jax.ShapeDtypeStruct((TOP_K, N), I32), jax.ShapeDtypeStruct((TOP_K, N), F32),
                   jax.ShapeDtypeStruct((TOP_K, N), I32), jax.ShapeDtypeStruct((E, LANES), F32)],
        scratch_shapes=[pltpu.VMEM((E, LANES), F32)],
        compiler_params=_cparams(("arbitrary",), 32),
        name="router",
    )(h2, router_wt, router_b, upper)


MOE_TM = 512


def _gather_rows_kernel(nu_ref, tok_ref, src_ref, o_ref, sem, *, tg):
    i = pl.program_id(0)

    @pl.when(i < nu_ref[0])
    def _():
        def issue(r, c):
            t = tok_ref[0, r]
            pltpu.make_async_copy(src_ref.at[pl.ds(t, 1)], o_ref.at[pl.ds(r, 1)], sem).start()
            return c

        lax.fori_loop(0, tg, issue, 0, unroll=8)

        def drain(r, c):
            pltpu.make_async_copy(src_ref.at[pl.ds(0, 1)], o_ref.at[pl.ds(0, 1)], sem).wait()
            return c

        lax.fori_loop(0, tg, drain, 0, unroll=8)

    @pl.when(i >= nu_ref[0])
    def _():
        o_ref[...] = jnp.zeros_like(o_ref)


def gather_rows(src, slot_tok, n_used):
    S = slot_tok.shape[0]
    W = src.shape[1]
    tg = MOE_TM
    nb = S // tg
    kern = functools.partial(_gather_rows_kernel, tg=tg)
    last = lambda i, nu: jnp.minimum(i, nu[0] - 1)
    return pl.pallas_call(
        kern,
        grid_spec=pltpu.PrefetchScalarGridSpec(
            num_scalar_prefetch=1,
            grid=(nb,),
            in_specs=[
                pl.BlockSpec((None, 1, tg), lambda i, nu: (last(i, nu), 0, 0), memory_space=pltpu.SMEM),
                pl.BlockSpec(memory_space=pl.ANY),
            ],
            out_specs=pl.BlockSpec((tg, W), lambda i, nu: (i, 0)),
            scratch_shapes=[pltpu.SemaphoreType.DMA(())],
        ),
        out_shape=jax.ShapeDtypeStruct((S, W), src.dtype),
        compiler_params=_cparams(("arbitrary",), 32),
        name="gather_rows",
    )(n_used, slot_tok.reshape(nb, 1, tg), src)


def _expert_up_kernel(be_ref, nu_ref, xs_ref, wg_ref, wu_ref, h_ref, wgb_ref, wub_ref):
    i = pl.program_id(1)
    nu = nu_ref[0]
    half = xs_ref.shape[1]
    prev = be_ref[jnp.maximum(i - 1, 0)]
    fresh = (i == 0) | (be_ref[i] != prev)

    @pl.when((i < nu) & fresh)
    def _():
        wgb_ref[...] = wg_ref[...].astype(BF16)
        wub_ref[...] = wu_ref[...].astype(BF16)

    @pl.when(i < nu)
    def _():
        u = xs_ref[...]
        lo = pltpu.bitcast(u << 16, F32).astype(BF16)
        hi = pltpu.bitcast(u & jnp.uint32(0xFFFF0000), F32).astype(BF16)
        g = _dot(lo, wgb_ref[0:half, :]) + _dot(hi, wgb_ref[half:2 * half, :])
        up = _dot(lo, wub_ref[0:half, :]) + _dot(hi, wub_ref[half:2 * half, :])
        h_ref[...] = (_silu(g) * up).astype(h_ref.dtype)

    @pl.when(i >= nu)
    def _():
        h_ref[...] = jnp.zeros_like(h_ref)


def expert_up(xs, blk_e, n_used, w_gate, w_up):
    S, half = xs.shape
    E, D, Fd = w_gate.shape
    tm, tn = MOE_TM, 256
    nb = S // tm
    row = lambda j, i, be, nu: jnp.minimum(i, nu[0] - 1)
    return pl.pallas_call(
        _expert_up_kernel,
        grid_spec=pltpu.PrefetchScalarGridSpec(
            num_scalar_prefetch=2,
            grid=(Fd // tn, nb),
            in_specs=[
                pl.BlockSpec((tm, half), lambda j, i, be, nu: (row(j, i, be, nu), 0)),
                pl.BlockSpec((None, D, tn), lambda j, i, be, nu: (be[row(j, i, be, nu)], 0, j)),
                pl.BlockSpec((None, D, tn), lambda j, i, be, nu: (be[row(j, i, be, nu)], 0, j)),
            ],
            out_specs=pl.BlockSpec((tm, tn), lambda j, i, be, nu: (i, j)),
            scratch_shapes=[pltpu.VMEM((D, tn), BF16), pltpu.VMEM((D, tn), BF16)],
        ),
        out_shape=jax.ShapeDtypeStruct((S, Fd), BF16),
        compiler_params=_cparams(("arbitrary", "arbitrary"), 48),
        name="expert_up",
    )(blk_e, n_used, xs, w_gate, w_up)


def _expert_down_kernel(be_ref, nu_ref, h_ref, wd_ref, y_ref, wdb_ref):
    i = pl.program_id(1)
    nu = nu_ref[0]
    prev = be_ref[jnp.maximum(i - 1, 0)]
    fresh = (i == 0) | (be_ref[i] != prev)

    @pl.when((i < nu) & fresh)
    def _():
        wdb_ref[...] = wd_ref[...].astype(BF16)

    @pl.when(i < nu)
    def _():
        y_ref[...] = _dot(h_ref[...], wdb_ref[...])

    @pl.when(i >= nu)
    def _():
        y_ref[...] = jnp.zeros_like(y_ref)


def expert_down(h, blk_e, n_used, w_down):
    S, Fd = h.shape
    E, _, D = w_down.shape
    tm, tn = MOE_TM, 1024
    nb = S // tm
    row = lambda j, i, be, nu: jnp.minimum(i, nu[0] - 1)
    return pl.pallas_call(
        _expert_down_kernel,
        grid_spec=pltpu.PrefetchScalarGridSpec(
            num_scalar_prefetch=2,
            grid=(D // tn, nb),
            in_specs=[
                pl.BlockSpec((tm, Fd), lambda j, i, be, nu: (row(j, i, be, nu), 0)),
                pl.BlockSpec((None, Fd, tn), lambda j, i, be, nu: (be[row(j, i, be, nu)], 0, j)),
            ],
            out_specs=pl.BlockSpec((tm, tn), lambda j, i, be, nu: (i, j)),
            scratch_shapes=[pltpu.VMEM((Fd, tn), BF16)],
        ),
        out_shape=jax.ShapeDtypeStruct((S, D), F32),
        compiler_params=_cparams(("arbitrary", "arbitrary"), 32),
        name="expert_down",
    )(blk_e, n_used, h, w_down)


def _combine_kernel(dest_ref, y_ref, w_ref, sh_ref, x1_ref, g_ref, mod_ref, o_ref, buf_ref, sem, *, tb):
    def issue(r, c):
        for k in range(TOP_K):
            d = dest_ref[0, k * tb + r]
            pltpu.make_async_copy(y_ref.at[pl.ds(d, 1)], buf_ref.at[k, pl.ds(r, 1)], sem).start()
        return c

    lax.fori_loop(0, tb, issue, 0, unroll=2)

    def drain(r, c):
        for k in range(TOP_K):
            pltpu.make_async_copy(y_ref.at[pl.ds(0, 1)], buf_ref.at[k, pl.ds(0, 1)], sem).wait()
        return c

    lax.fori_loop(0, tb, drain, 0, unroll=2)
    w = w_ref[...]
    acc = sh_ref[...]
    for k in range(TOP_K):
        acc = acc + w[:, k:k + 1] * buf_ref[k]
    o_ref[...] = x1_ref[...] + mod_ref[5:6, :] * (_rms(acc) * g_ref[3:4, :])


def combine(y_sorted, dest, w, shared, x1, g, mods, n_ctx, t_lat):
    N, D = x1.shape
    tb = 128
    nb = N // tb
    dest3 = dest.reshape(TOP_K, nb, tb).transpose(1, 0, 2).reshape(nb, 1, TOP_K * tb)
    kern = functools.partial(_combine_kernel, tb=tb)
    return pl.pallas_call(
        kern,
        grid=(nb,),
        in_specs=[
            pl.BlockSpec((None, 1, TOP_K * tb), lambda i: (i, 0, 0), memory_space=pltpu.SMEM),
            pl.BlockSpec(memory_space=pl.ANY),
            pl.BlockSpec((tb, TOP_K), lambda i: (i, 0)),
            pl.BlockSpec((tb, D), lambda i: (i, 0)),
            pl.BlockSpec((tb, D), lambda i: (i, 0)),
            pl.BlockSpec((4, D), lambda i: (0, 0)),
            pl.BlockSpec((None, 6, D), lambda i: (_mod_row_index(i, tb, n_ctx, t_lat), 0, 0)),
        ],
        out_specs=pl.BlockSpec((tb, D), lambda i: (i, 0)),
        out_shape=jax.ShapeDtypeStruct((N, D), F32),
        scratch_shapes=[pltpu.VMEM((TOP_K, tb, D), F32), pltpu.SemaphoreType.DMA(())],
        compiler_params=_cparams(("arbitrary",), 48),
        name="combine",
    )(dest3, y_sorted, w, shared, x1, g, mods)


def moe_ffn(h2, hp, x1, p, g, mods, n_ctx, t_lat):
    N, D = x1.shape
    tm = MOE_TM
    idx, w, rank, cnt = router(h2, p['router_wt'], p['router_b'])
    counts = cnt[:, 0].astype(I32)
    padded = (counts + tm - 1) // tm * tm
    pend = jnp.cumsum(padded)
    pstart = pend - padded
    n_blocks = (N * TOP_K + N_EXPERTS * (tm - 1) + tm - 1) // tm
    n_used = (pend[-1] // tm).astype(I32).reshape(1)
    blk_e = jnp.minimum(jnp.searchsorted(pend, jnp.arange(n_blocks, dtype=I32) * tm, side='right'),
                        N_EXPERTS - 1).astype(I32)
    dest = pstart[idx] + rank
    tok = jnp.broadcast_to(jnp.arange(N, dtype=I32)[None, :], (TOP_K, N))
    slot_tok = jnp.zeros((n_blocks * tm,), I32).at[dest.reshape(-1)].set(tok.reshape(-1))
    xs = gather_rows(hp, slot_tok, n_used)
    hid = expert_up(xs, blk_e, n_used, p['exp_w_gate'], p['exp_w_up'])
    ys = expert_down(hid, blk_e, n_used, p['exp_w_down'])
    sh_hid = swiglu(h2, p['sh_w_gate'], p['sh_w_up'])
    shared = matmul(sh_hid, p['sh_w_down'])
    return combine(ys, dest, w.T, shared, x1, g, mods, n_ctx, t_lat)


def _axial_rope_tables(n):
    rows = n // GRID_W
    t = jnp.arange(rows * GRID_W)
    row = (t // GRID_W).astype(F32)
    col = (t % GRID_W).astype(F32)
    n_freq = LANES // 4
    inv = ROPE_BASE ** (-jnp.arange(n_freq, dtype=F32) / n_freq)
    ang = jnp.concatenate([row[:, None] * inv, col[:, None] * inv], axis=-1)
    cos, sin = jnp.cos(ang), jnp.sin(ang)
    return jnp.concatenate([cos, cos], axis=-1), jnp.concatenate([-sin, sin], axis=-1)


def _layer_params(l, w_in, conv_w, conv_b, lru_wa, lru_ba, lru_wx, lru_bx, lru_lam, diff_lam, diff_subln,
                  gla_w, gla_b, gla_norm, w_out, router_w, router_b, exp_w_gate, exp_w_up, exp_w_down,
                  sh_w_gate, sh_w_up, sh_w_down):
    d_lru = conv_w.shape[-1]
    n_main = w_in.shape[-1] - 2 * GLA_RANK
    dk = LANES
    wa, wx = lru_wa[l], lru_wx[l]
    w_gate = jnp.concatenate([wa[0], wx[0], wa[1], wx[1]], axis=-1).astype(BF16)
    ba = lru_ba[l].reshape(2, H_LRU, 1, -1)
    bx = lru_bx[l].reshape(2, H_LRU, 1, -1)
    b_gate = jnp.concatenate([ba[0], bx[0], ba[1], bx[1]], axis=-1)
    gw = gla_w[l]
    w_glog = jnp.zeros((H_GLA, LANES, 2 * dk), F32)
    for d in range(2):
        blk = gw[d].reshape(GLA_RANK, H_GLA, dk).transpose(1, 0, 2)
        w_glog = w_glog.at[:, d * GLA_RANK:(d + 1) * GLA_RANK, d * dk:(d + 1) * dk].set(blk)
    gb = gla_b[l].reshape(2, H_GLA, 1, dk)
    b_glog = jnp.concatenate([gb[0], gb[1]], axis=-1)
    w_lr = jnp.zeros((w_in.shape[1], LANES), F32).at[:, :2 * GLA_RANK].set(w_in[l][:, n_main:])
    return {
        'w_main': w_in[l][:, :n_main].astype(BF16), 'w_lr': w_lr.astype(BF16),
        'conv_w': conv_w[l], 'conv_b': conv_b[l].reshape(1, d_lru),
        'w_gate': w_gate, 'b_gate': b_gate, 'lru_lam': lru_lam[l],
        'diff_lam': diff_lam[l], 'diff_subln': diff_subln[l].reshape(1, -1),
        'w_glog': w_glog.astype(BF16), 'b_glog': b_glog, 'gla_norm': gla_norm[l].reshape(1, -1),
        'w_out': w_out[l].astype(BF16),
        'router_wt': router_w[l].T.astype(BF16),
        'router_b': jnp.broadcast_to(router_b[l][:, None], (N_EXPERTS, LANES)),
        'exp_w_gate': exp_w_gate[l], 'exp_w_up': exp_w_up[l], 'exp_w_down': exp_w_down[l],
        'sh_w_gate': sh_w_gate[l].astype(BF16), 'sh_w_up': sh_w_up[l].astype(BF16),
        'sh_w_down': sh_w_down[l].astype(BF16),
    }


def kernel(x_prompt, x_sample, cache_k, cache_v, state_lru, state_gla, c, c_ctx, mod_w, mod_b, norm_g, w_in,
           conv_w, conv_b, lru_wa, lru_ba, lru_wx, lru_bx, lru_lam, diff_lam, diff_subln, gla_w, gla_b,
           gla_norm, w_out, router_w, router_b, exp_w_gate, exp_w_up, exp_w_down, sh_w_gate, sh_w_up,
           sh_w_down):
    Bc, Tc, D = x_prompt.shape
    Bl, Tl, _ = x_sample.shape
    L = mod_w.shape[0]
    n_ctx, n_lat = Bc * Tc, Bl * Tl
    d_lru = conv_w.shape[-1]
    d_diff = H_DIFF * 2 * LANES
    d_gk = H_GLA * LANES
    d_gv = H_GLA * 2 * LANES
    col_qb = 2 * d_lru
    col_qc = col_qb + 3 * d_diff
    col_kc = col_qc + d_gk
    col_vc = col_kc + d_gk
    col_gc = col_vc + d_gv

    x = jnp.concatenate([x_prompt.reshape(n_ctx, D), x_sample.reshape(n_lat, D)], axis=0)
    n_rows = 16 * ((1 + Bl + 15) // 16)
    c_rows = jnp.zeros((n_rows, D), F32).at[0].set(c_ctx).at[1:1 + Bl].set(c)
    mods_all = modulation(c_rows, mod_w, mod_b).reshape(L, n_rows, 6, D)
    rope = _axial_rope_tables(Tl)
    cache_k2 = cache_k.reshape(*cache_k.shape[:3], -1)
    cache_v2 = cache_v.reshape(*cache_v.shape[:3], -1)
    zeros_lru = jnp.zeros((Bc, 2, d_lru), F32)
    zeros_gla = jnp.zeros((Bc, 2, H_GLA, LANES, 2 * LANES), F32)

    ks, vs, lrus, glas = [], [], [], []
    for l in range(L):
        p = _layer_params(l, w_in, conv_w, conv_b, lru_wa, lru_ba, lru_wx, lru_bx, lru_lam, diff_lam,
                          diff_subln, gla_w, gla_b, gla_norm, w_out, router_w, router_b, exp_w_gate,
                          exp_w_up, exp_w_down, sh_w_gate, sh_w_up, sh_w_down)
        lam_init = 0.8 - 0.6 * math.exp(-0.3 * l)
        mods = mods_all[l]
        g = norm_g[l]
        h = norm_mod(x, g[0:1], mods, n_ctx, Tl)
        z = matmul(h, p['w_main'])
        z_lr = matmul(h, p['w_lr'])
        ya_c, lru_c = lru_mixer(z, 0, Bc, Tc, p['conv_w'], p['conv_b'], p['w_gate'], p['b_gate'],
                                p['lru_lam'], zeros_lru)
        ya_l, _ = lru_mixer(z, n_ctx, Bl, Tl, p['conv_w'], p['conv_b'], p['w_gate'], p['b_gate'],
                            p['lru_lam'], state_lru[:, l])
        yb_c = diff_attention(z, 0, Bc, Tc, col_qb, None, None, l, None, p['diff_lam'], p['diff_subln'], lam_init)
        yb_l = diff_attention(z, n_ctx, Bl, Tl, col_qb, cache_k2, cache_v2, l, rope, p['diff_lam'],
                              p['diff_subln'], lam_init)
        yc_c, gla_c = gla_mixer(z, z_lr, 0, Bc, Tc, col_qc, col_kc, col_vc, col_gc, p['w_glog'], p['b_glog'],
                                p['gla_norm'], zeros_gla)
        yc_l, _ = gla_mixer(z, z_lr, n_ctx, Bl, Tl, col_qc, col_kc, col_vc, col_gc, p['w_glog'], p['b_glog'],
                            p['gla_norm'], state_gla[:, l])
        ycat = jnp.concatenate([jnp.concatenate([ya_c, yb_c, yc_c], axis=1),
                                jnp.concatenate([ya_l, yb_l, yc_l], axis=1)], axis=0)
        y = matmul(ycat, p['w_out'])
        x1, h2, hp = post_mix(x, y, g, mods, n_ctx, Tl)
        x = moe_ffn(h2, hp, x1, p, g, mods, n_ctx, Tl)
        ks.append(z[:n_ctx, col_qb + d_diff:col_qb + 2 * d_diff].reshape(Bc, Tc, H_DIFF, 2 * LANES))
        vs.append(z[:n_ctx, col_qb + 2 * d_diff:col_qb + 3 * d_diff].reshape(Bc, Tc, H_DIFF, 2 * LANES))
        lrus.append(lru_c)
        glas.append(gla_c)
    yp = x[:n_ctx].reshape(Bc, Tc, D)
    ys = x[n_ctx:].reshape(Bl, Tl, D)
    return (yp, ys, jnp.stack(ks, axis=1), jnp.stack(vs, axis=1), jnp.stack(lrus, axis=1), jnp.stack(glas, axis=1))
```

```python
import functools
import math

import jax
import jax.numpy as jnp
from jax import lax
from jax.experimental import pallas as pl
from jax.experimental.pallas import tpu as pltpu

F32 = jnp.float32
BF16 = jnp.bfloat16
U32 = jnp.uint32
I32 = jnp.int32

EPS = 1e-6
GRID_W = 64
H_LRU = 8
CONV_W = 4
CONV_LEFT = 2
LRU_C = 8.0
H_DIFF = 8
ROPE_BASE = 10000.0
H_GLA = 4
GLA_RANK = 16
GLA_TAU = 16.0
GLA_CHUNK = 64
N_EXPERTS = 64
TOP_K = 8
N_GROUPS = 8
TOPK_GROUPS = 4
ROUTED_SCALE = 2.5

V7X_VMEM_BYTES = 64 * 1024 * 1024
LANES = 128
SUBLANES = 8
MIB = 1024 * 1024


def _cparams(semantics, vmem_mib):
    assert vmem_mib * MIB < V7X_VMEM_BYTES
    return pltpu.CompilerParams(dimension_semantics=semantics, vmem_limit_bytes=vmem_mib * MIB)


def _rms(x):
    return x * lax.rsqrt(jnp.mean(x * x, axis=-1, keepdims=True) + EPS)


def _sigmoid(x):
    return 1.0 / (1.0 + jnp.exp(-x))


def _silu(x):
    return x * _sigmoid(x)


def _dot(a, b):
    return jnp.dot(a, b, preferred_element_type=F32)


def _dot_nt(a, b):
    return lax.dot_general(a, b, (((1,), (1,)), ((), ())), preferred_element_type=F32)


def _dot_tn(a, b):
    return lax.dot_general(a, b, (((0,), (0,)), ((), ())), preferred_element_type=F32)


def _mod_kernel(c_ref, w_ref, b_ref, o_ref):
    s = _silu(c_ref[...]).astype(BF16)
    o_ref[...] = _dot(s, w_ref[...].astype(BF16)) + b_ref[...]


def modulation(c_rows, mod_w, mod_b):
    L, D, N6 = mod_w.shape
    R = c_rows.shape[0]
    tn = 512
    return pl.pallas_call(
        _mod_kernel,
        grid=(L, N6 // tn),
        in_specs=[
            pl.BlockSpec((R, D), lambda l, j: (0, 0)),
            pl.BlockSpec((None, D, tn), lambda l, j: (l, 0, j)),
            pl.BlockSpec((None, 1, tn), lambda l, j: (l, 0, j)),
        ],
        out_specs=pl.BlockSpec((None, R, tn), lambda l, j: (l, 0, j)),
        out_shape=jax.ShapeDtypeStruct((L, R, N6), F32),
        compiler_params=_cparams(("arbitrary", "arbitrary"), 40),
        name="modulation",
    )(c_rows, mod_w, mod_b.reshape(L, 1, N6))


def _mod_row_index(i, tm, n_ctx, t_lat):
    r0 = i * tm
    return jnp.where(r0 < n_ctx, 0, 1 + (r0 - n_ctx) // t_lat)


def _norm_mod_kernel(x_ref, g_ref, mod_ref, o_ref):
    y = _rms(x_ref[...]) * g_ref[...]
    o_ref[...] = (y * (1.0 + mod_ref[1:2, :]) + mod_ref[0:1, :]).astype(o_ref.dtype)


def norm_mod(x, g, mods, n_ctx, t_lat):
    N, D = x.shape
    tm = 256
    return pl.pallas_call(
        _norm_mod_kernel,
        grid=(N // tm,),
        in_specs=[
            pl.BlockSpec((tm, D), lambda i: (i, 0)),
            pl.BlockSpec((1, D), lambda i: (0, 0)),
            pl.BlockSpec((None, 6, D), lambda i: (_mod_row_index(i, tm, n_ctx, t_lat), 0, 0)),
        ],
        out_specs=pl.BlockSpec((tm, D), lambda i: (i, 0)),
        out_shape=jax.ShapeDtypeStruct((N, D), BF16),
        compiler_params=_cparams(("arbitrary",), 32),
        name="norm_mod",
    )(x, g, mods)


def _mm_kernel(x_ref, w_ref, o_ref):
    o_ref[...] = _dot(x_ref[...], w_ref[...]).astype(o_ref.dtype)


def matmul(x, w, out_dtype=F32, tm=1024, tn=512):
    M, K = x.shape
    N = w.shape[1]
    tm = min(tm, M)
    tn = min(tn, N)
    assert M % tm == 0 and N % tn == 0
    return pl.pallas_call(
        _mm_kernel,
        grid=(M // tm, N // tn),
        in_specs=[pl.BlockSpec((tm, K), lambda i, j: (i, 0)), pl.BlockSpec((K, tn), lambda i, j: (0, j))],
        out_specs=pl.BlockSpec((tm, tn), lambda i, j: (i, j)),
        out_shape=jax.ShapeDtypeStruct((M, N), out_dtype),
        compiler_params=_cparams(("arbitrary", "arbitrary"), 48),
        name="matmul",
    )(x, w)


def _swiglu_kernel(x_ref, wg_ref, wu_ref, o_ref):
    x = x_ref[...]
    o_ref[...] = (_silu(_dot(x, wg_ref[...])) * _dot(x, wu_ref[...])).astype(o_ref.dtype)


def swiglu(x, wg, wu, tm=1024, tn=256):
    M, K = x.shape
    N = wg.shape[1]
    tm = min(tm, M)
    return pl.pallas_call(
        _swiglu_kernel,
        grid=(M // tm, N // tn),
        in_specs=[
            pl.BlockSpec((tm, K), lambda i, j: (i, 0)),
            pl.BlockSpec((K, tn), lambda i, j: (0, j)),
            pl.BlockSpec((K, tn), lambda i, j: (0, j)),
        ],
        out_specs=pl.BlockSpec((tm, tn), lambda i, j: (i, j)),
        out_shape=jax.ShapeDtypeStruct((M, N), BF16),
        compiler_params=_cparams(("arbitrary", "arbitrary"), 48),
        name="swiglu",
    )(x, wg, wu)


def _lru_kernel(xa_ref, ga_ref, cw_ref, cb_ref, wg_ref, bg_ref, lam_ref, h0_ref, ybuf_ref, y_ref, hl_ref,
                pad_ref, af_ref, uf_ref, ab_ref, ub_ref, *, T, CW):
    del ybuf_ref
    pad_ref[0:SUBLANES, :] = jnp.zeros((SUBLANES, CW), F32)
    pad_ref[SUBLANES + T:2 * SUBLANES + T, :] = jnp.zeros((SUBLANES, CW), F32)
    pad_ref[SUBLANES:SUBLANES + T, :] = xa_ref[...]
    xc = cb_ref[...] + jnp.zeros((T, CW), F32)
    for j in range(CONV_W):
        o = SUBLANES - CONV_LEFT + j
        xc = xc + cw_ref[j:j + 1, :] * pad_ref[o:o + T, :]
    lam = lam_ref[...]
    nl = -lam
    softplus = jnp.maximum(nl, 0.0) + jnp.log(1.0 + jnp.exp(-jnp.abs(nl)))
    bs = LANES
    for hh in range(CW // bs):
        cs = slice(hh * bs, (hh + 1) * bs)
        xb = xc[:, cs]
        gts = _dot(xb.astype(BF16), wg_ref[hh]) + bg_ref[hh]
        for d, (a_ref, u_ref) in enumerate(((af_ref, uf_ref), (ab_ref, ub_ref))):
            r = _sigmoid(gts[:, (2 * d) * bs:(2 * d + 1) * bs])
            ig = _sigmoid(gts[:, (2 * d + 1) * bs:(2 * d + 2) * bs])
            log_a = (-LRU_C) * r * softplus[d:d + 1, cs]
            a = jnp.exp(log_a)
            a_ref[:, cs] = a
            u_ref[:, cs] = jnp.sqrt(1.0 - a * a) * (ig * xb)

    def step(t, carry):
        hf, hb = carry
        hf = af_ref[pl.ds(t, 1), :] * hf + uf_ref[pl.ds(t, 1), :]
        uf_ref[pl.ds(t, 1), :] = hf
        tb = T - 1 - t
        hb = ab_ref[pl.ds(tb, 1), :] * hb + ub_ref[pl.ds(tb, 1), :]
        ub_ref[pl.ds(tb, 1), :] = hb
        return hf, hb

    hf, hb = lax.fori_loop(0, T, step, (h0_ref[0:1, :], h0_ref[1:2, :]), unroll=8)
    hl_ref[0:1, :] = hf
    hl_ref[1:2, :] = hb
    y_ref[...] = ((uf_ref[...] + ub_ref[...]) * jax.nn.gelu(ga_ref[...], approximate=True)).astype(y_ref.dtype)


def lru_mixer(z, ybuf, row0, B, T, conv_w, conv_b, w_gate, b_gate, lam, h0):
    d_lru = conv_w.shape[1]
    CW = 512
    nj = d_lru // CW
    rb0 = row0 // T
    kern = functools.partial(_lru_kernel, T=T, CW=CW)
    return pl.pallas_call(
        kern,
        grid=(B, nj),
        in_specs=[
            pl.BlockSpec((T, CW), lambda b, j: (rb0 + b, j)),
            pl.BlockSpec((T, CW), lambda b, j: (rb0 + b, nj + j)),
            pl.BlockSpec((CONV_W, CW), lambda b, j: (0, j)),
            pl.BlockSpec((1, CW), lambda b, j: (0, j)),
            pl.BlockSpec((CW // LANES, LANES, 4 * LANES), lambda b, j: (j, 0, 0)),
            pl.BlockSpec((CW // LANES, 1, 4 * LANES), lambda b, j: (j, 0, 0)),
            pl.BlockSpec((2, CW), lambda b, j: (0, j)),
            pl.BlockSpec((None, 2, CW), lambda b, j: (b, 0, j)),
            pl.BlockSpec(memory_space=pl.ANY),
        ],
        out_specs=[
            pl.BlockSpec((T, CW), lambda b, j: (rb0 + b, j)),
            pl.BlockSpec((None, 2, CW), lambda b, j: (b, 0, j)),
        ],
        out_shape=[jax.ShapeDtypeStruct(ybuf.shape, ybuf.dtype), jax.ShapeDtypeStruct((B, 2, d_lru), F32)],
        scratch_shapes=[pltpu.VMEM((T + 2 * SUBLANES, CW), F32)] + [pltpu.VMEM((T, CW), F32)] * 4,
        input_output_aliases={8: 0},
        compiler_params=_cparams(("arbitrary", "arbitrary"), 40),
        name="lru_mixer",
    )(z, z, conv_w, conv_b, w_gate, b_gate, lam, h0, ybuf)


def _rope(x, cos2, sin2):
    return x * cos2 + pltpu.roll(x, LANES // 2, axis=1) * sin2


def _attn_kernel(*refs, T, P, tq, use_rope, lam_init):
    it = iter(refs)
    q_ref, k_ref, v_ref = next(it), next(it), next(it)
    ck_ref = cv_ref = cos_ref = sin_ref = None
    if P:
        ck_ref, cv_ref = next(it), next(it)
    if use_rope:
        cos_ref, sin_ref = next(it), next(it)
    dl_ref, sub_ref, _, o_ref, kall_ref, vall_ref = next(it), next(it), next(it), next(it), next(it), next(it)
    dh = LANES
    qi = pl.program_id(2)

    @pl.when(qi == 0)
    def _():
        if P:
            kall_ref[0:P, :] = ck_ref[...].astype(BF16)
            vall_ref[0:P, :] = cv_ref[...].astype(BF16)
        for m in range(2):
            km = k_ref[:, m * dh:(m + 1) * dh]
            if use_rope:
                km = _rope(km, cos_ref[...], sin_ref[...])
            kall_ref[P:P + T, m * dh:(m + 1) * dh] = km.astype(BF16)
        vall_ref[P:P + T, :] = v_ref[...].astype(BF16)

    dl = dl_ref[...]
    lam = (jnp.exp(jnp.sum(dl[0:1, :] * dl[1:2, :], axis=-1, keepdims=True))
           - jnp.exp(jnp.sum(dl[2:3, :] * dl[3:4, :], axis=-1, keepdims=True)) + lam_init)
    qscale = (dh ** -0.5) * math.log2(math.e)
    pv, invs = [], []
    for m in range(2):
        qm = q_ref[:, m * dh:(m + 1) * dh]
        if use_rope:
            r0 = pl.multiple_of(qi * tq, tq)
            qm = _rope(qm, cos_ref[pl.ds(r0, tq), :], sin_ref[pl.ds(r0, tq), :])
        s2 = _dot_nt((qm * qscale).astype(BF16), kall_ref[:, m * dh:(m + 1) * dh])
        e = jnp.exp2(s2 - jnp.max(s2, axis=-1, keepdims=True))
        invs.append(1.0 / jnp.sum(e, axis=-1, keepdims=True))
        pv.append(_dot(e.astype(BF16), vall_ref[...]))
    o = pv[0] * invs[0] - pv[1] * (lam * invs[1])
    o_ref[...] = (_rms(o) * sub_ref[...] * (1.0 - lam_init)).astype(o_ref.dtype)


def diff_attention(z, ybuf, col_y, row0, B, T, col_q, cache_k, cache_v, layer, rope, diff_lam, subln, lam_init):
    H = H_DIFF
    dw = 2 * LANES
    P = 0 if cache_k is None else cache_k.shape[2]
    tq = min(T, 256)
    nq = T // tq
    rbq = row0 // tq
    rbk = row0 // T
    cq = col_q // dw
    kern = functools.partial(_attn_kernel, T=T, P=P, tq=tq, use_rope=rope is not None, lam_init=lam_init)
    in_specs = [
        pl.BlockSpec((tq, dw), lambda b, h, i: (rbq + b * nq + i, cq + h)),
        pl.BlockSpec((T, dw), lambda b, h, i: (rbk + b, cq + H + h)),
        pl.BlockSpec((T, dw), lambda b, h, i: (rbk + b, cq + 2 * H + h)),
    ]
    args = [z, z, z]
    if P:
        in_specs += [pl.BlockSpec((None, None, P, dw), lambda b, h, i: (b, layer, 0, h))] * 2
        args += [cache_k, cache_v]
    if rope is not None:
        in_specs += [pl.BlockSpec((T, LANES), lambda b, h, i: (0, 0))] * 2
        args += [rope[0], rope[1]]
    in_specs += [pl.BlockSpec((4, LANES), lambda b, h, i: (0, 0)), pl.BlockSpec((1, dw), lambda b, h, i: (0, 0)),
                 pl.BlockSpec(memory_space=pl.ANY)]
    args += [diff_lam, subln, ybuf]
    cy = col_y // dw
    return pl.pallas_call(
        kern,
        grid=(B, H, nq),
        in_specs=in_specs,
        out_specs=pl.BlockSpec((tq, dw), lambda b, h, i: (rbq + b * nq + i, cy + h)),
        out_shape=jax.ShapeDtypeStruct(ybuf.shape, ybuf.dtype),
        scratch_shapes=[pltpu.VMEM((P + T, dw), BF16), pltpu.VMEM((P + T, dw), BF16)],
        input_output_aliases={len(args) - 1: 0},
        compiler_params=_cparams(("arbitrary", "arbitrary", "arbitrary"), 40),
        name="diff_attention",
    )(*args)


def _gla_kernel(q_ref, k_ref, v_ref, gc_ref, lr_ref, wgl_ref, bgl_ref, gn_ref, s0_ref, ybuf_ref, y_ref, so_ref,
                glog_ref, sf_ref, sb_ref, of_ref, ob_ref, *, T):
    del ybuf_ref
    C = GLA_CHUNK
    dk = LANES
    nc = T // C
    gl = _dot(lr_ref[...].astype(BF16), wgl_ref[...]) + bgl_ref[...]
    glog_ref[...] = (jnp.minimum(gl, 0.0) - jnp.log(1.0 + jnp.exp(-jnp.abs(gl)))) * (1.0 / GLA_TAU)
    row = lax.broadcasted_iota(I32, (C, C), 0)
    col = lax.broadcasted_iota(I32, (C, C), 1)
    qscale = dk ** -0.5
    ones = jnp.ones((C, dk), BF16)
    sf_ref[...] = s0_ref[0]
    sb_ref[...] = s0_ref[1]

    def one_chunk(c, d, s_ref, o_ref):
        keep = (row >= col) if d == 0 else (row <= col)
        tri = jnp.where(keep, 1.0, 0.0).astype(BF16)
        r0 = pl.multiple_of(c * C, C)
        g = glog_ref[pl.ds(r0, C), d * dk:(d + 1) * dk]
        g_hi = g.astype(BF16)
        g_lo = (g - g_hi.astype(F32)).astype(BF16)
        b = _dot(tri, g_hi) + _dot(tri, g_lo)
        b_end = b[C - 1:C, :] if d == 0 else b[0:1, :]
        q = q_ref[pl.ds(r0, C), :] * qscale
        k = k_ref[pl.ds(r0, C), :]
        v = v_ref[pl.ds(r0, C), :].astype(BF16)
        qt = (q * jnp.exp(b)).astype(BF16)
        kt = (k * jnp.exp(-b)).astype(BF16)
        att = jnp.where(keep, _dot_nt(qt, kt), 0.0)
        S = s_ref[...]
        o_ref[pl.ds(r0, C), :] = _dot(att.astype(BF16), v) + _dot(qt, S.astype(BF16))
        kd = (k * jnp.exp(b_end - b)).astype(BF16)
        bcol = _dot_tn(g_hi, ones) + _dot_tn(g_lo, ones)
        ecol = jnp.exp(bcol)
        s_ref[...] = jnp.concatenate([ecol, ecol], axis=1) * S + _dot_tn(kd, v)

    def chunk(ci, carry):
        one_chunk(ci, 0, sf_ref, of_ref)
        one_chunk(nc - 1 - ci, 1, sb_ref, ob_ref)
        return carry

    lax.fori_loop(0, nc, chunk, 0)
    so_ref[0] = sf_ref[...]
    so_ref[1] = sb_ref[...]
    oc = of_ref[...] + ob_ref[...]
    y_ref[...] = (_rms(oc) * gn_ref[...] * _silu(gc_ref[...])).astype(y_ref.dtype)


def gla_mixer(z, z_lr, ybuf, col_y, row0, B, T, col_q, col_k, col_v, col_g, w_glog, b_glog, gnorm, s0):
    H = H_GLA
    dk, dv = LANES, 2 * LANES
    rb = row0 // T
    kern = functools.partial(_gla_kernel, T=T)
    return pl.pallas_call(
        kern,
        grid=(B, H),
        in_specs=[
            pl.BlockSpec((T, dk), lambda b, h: (rb + b, col_q // dk + h)),
            pl.BlockSpec((T, dk), lambda b, h: (rb + b, col_k // dk + h)),
            pl.BlockSpec((T, dv), lambda b, h: (rb + b, col_v // dv + h)),
            pl.BlockSpec((T, dv), lambda b, h: (rb + b, col_g // dv + h)),
            pl.BlockSpec((T, LANES), lambda b, h: (rb + b, 0)),
            pl.BlockSpec((None, LANES, 2 * dk), lambda b, h: (h, 0, 0)),
            pl.BlockSpec((None, 1, 2 * dk), lambda b, h: (h, 0, 0)),
            pl.BlockSpec((1, dv), lambda b, h: (0, 0)),
            pl.BlockSpec((None, 2, None, dk, dv), lambda b, h: (b, 0, h, 0, 0)),
            pl.BlockSpec(memory_space=pl.ANY),
        ],
        out_specs=[
            pl.BlockSpec((T, dv), lambda b, h: (rb + b, col_y // dv + h)),
            pl.BlockSpec((None, 2, None, dk, dv), lambda b, h: (b, 0, h, 0, 0)),
        ],
        out_shape=[jax.ShapeDtypeStruct(ybuf.shape, ybuf.dtype), jax.ShapeDtypeStruct((B, 2, H, dk, dv), F32)],
        scratch_shapes=[pltpu.VMEM((T, 2 * dk), F32), pltpu.VMEM((dk, dv), F32), pltpu.VMEM((dk, dv), F32),
                        pltpu.VMEM((T, dv), F32), pltpu.VMEM((T, dv), F32)],
        input_output_aliases={9: 0},
        compiler_params=_cparams(("arbitrary", "arbitrary"), 32),
        name="gla_mixer",
    )(z, z, z, z, z_lr, w_glog, b_glog, gnorm, s0, ybuf)


def _post_mix_kernel(x_ref, y_ref, g_ref, mod_ref, x1_ref, h2_ref, hp_ref):
    D = x_ref.shape[1]
    x1 = x_ref[...] + mod_ref[2:3, :] * (_rms(y_ref[...]) * g_ref[1:2, :])
    x1_ref[...] = x1
    h = (_rms(x1) * g_ref[2:3, :]) * (1.0 + mod_ref[4:5, :]) + mod_ref[3:4, :]
    hb = h.astype(BF16)
    h2_ref[...] = hb
    u = pltpu.bitcast(hb.astype(F32), U32)
    hp_ref[...] = (u[:, :D // 2] >> 16) | (u[:, D // 2:] & jnp.uint32(0xFFFF0000))


def post_mix(x, y, g, mods, n_ctx, t_lat):
    N, D = x.shape
    tm = 256
    return pl.pallas_call(
        _post_mix_kernel,
        grid=(N // tm,),
        in_specs=[
            pl.BlockSpec((tm, D), lambda i: (i, 0)),
            pl.BlockSpec((tm, D), lambda i: (i, 0)),
            pl.BlockSpec((4, D), lambda i: (0, 0)),
            pl.BlockSpec((None, 6, D), lambda i: (_mod_row_index(i, tm, n_ctx, t_lat), 0, 0)),
        ],
        out_specs=[
            pl.BlockSpec((tm, D), lambda i: (i, 0)),
            pl.BlockSpec((tm, D), lambda i: (i, 0)),
            pl.BlockSpec((tm, D // 2), lambda i: (i, 0)),
        ],
        out_shape=[jax.ShapeDtypeStruct((N, D), F32), jax.ShapeDtypeStruct((N, D), BF16),
                   jax.ShapeDtypeStruct((N, D // 2), U32)],
        compiler_params=_cparams(("arbitrary",), 48),
        name="post_mix",
    )(x, y, g, mods)


def _router_kernel(h_ref, rwt_ref, rb_ref, upper_ref, idx_ref, w_ref, rank_ref, cnt_ref, carry_ref, *, tm):
    i = pl.program_id(0)
    per = N_EXPERTS // N_GROUPS

    @pl.when(i == 0)
    def _():
        carry_ref[...] = jnp.zeros_like(carry_ref)

    logits = _dot_nt(rwt_ref[...], h_ref[...])
    s_all = _sigmoid(logits)
    sel_all = s_all + rb_ref[...][:, 0:1]
    sub = lax.broadcasted_iota(I32, (per, tm), 0)
    neg = -jnp.inf
    s_g, sel_g, gscore = [], [], []
    for g in range(N_GROUPS):
        blk = sel_all[g * per:(g + 1) * per, :]
        s_g.append(s_all[g * per:(g + 1) * per, :])
        sel_g.append(blk)
        m1 = jnp.max(blk, axis=0, keepdims=True)
        first = jnp.min(jnp.where(blk == m1, sub, per), axis=0, keepdims=True)
        m2 = jnp.max(jnp.where(sub == first, neg, blk), axis=0, keepdims=True)
        gscore.append(m1 + m2)
    for g in range(N_GROUPS):
        beat = jnp.zeros((1, tm), F32)
        for g2 in range(N_GROUPS):
            if g2 == g:
                continue
            wins = (gscore[g2] > gscore[g]) | ((gscore[g2] == gscore[g]) & (g2 < g))
            beat = beat + jnp.where(wins, 1.0, 0.0)
        sel_g[g] = jnp.where(beat < TOPK_GROUPS, sel_g[g], neg)
    eidx = [sub + g * per for g in range(N_GROUPS)]
    onehot = [jnp.zeros((per, tm), F32) for _ in range(N_GROUPS)]
    picks, wts = [], []
    for k in range(TOP_K):
        m = sel_g[0]
        for g in range(1, N_GROUPS):
            m = jnp.maximum(m, sel_g[g])
        m = jnp.max(m, axis=0, keepdims=True)
        cand = jnp.where(sel_g[0] == m, eidx[0], N_EXPERTS)
        for g in range(1, N_GROUPS):
            cand = jnp.minimum(cand, jnp.where(sel_g[g] == m, eidx[g], N_EXPERTS))
        pick = jnp.min(cand, axis=0, keepdims=True)
        wk = jnp.zeros((per, tm), F32)
        for g in range(N_GROUPS):
            hit = eidx[g] == pick
            wk = wk + jnp.where(hit, s_g[g], 0.0)
            onehot[g] = onehot[g] + jnp.where(hit, 1.0, 0.0)
            sel_g[g] = jnp.where(hit, neg, sel_g[g])
        picks.append(pick)
        wts.append(jnp.sum(wk, axis=0, keepdims=True))
    wsum = wts[0]
    for k in range(1, TOP_K):
        wsum = wsum + wts[k]
    oh = jnp.concatenate(onehot, axis=0)
    carry = carry_ref[...][:, 0:1]
    before = _dot(oh.astype(BF16), upper_ref[...]) + carry
    for k in range(TOP_K):
        idx_ref[k:k + 1, :] = picks[k]
        w_ref[k:k + 1, :] = wts[k] / wsum * ROUTED_SCALE
        rk = jnp.zeros((per, tm), F32)
        for g in range(N_GROUPS):
            rk = rk + jnp.where(eidx[g] == picks[k], before[g * per:(g + 1) * per, :], 0.0)
        rank_ref[k:k + 1, :] = jnp.sum(rk, axis=0, keepdims=True).astype(I32)
    carry_ref[...] = carry_ref[...] + jnp.sum(oh, axis=1, keepdims=True)
    cnt_ref[...] = carry_ref[...]


def router(h2, router_wt, router_b):
    N, D = h2.shape
    tm = 512 if N % 512 == 0 else 256
    E = N_EXPERTS
    upper = (jnp.arange(tm)[:, None] < jnp.arange(tm)[None, :]).astype(BF16)
    kern = functools.partial(_router_kernel, tm=tm)
    return pl.pallas_call(
        kern,
        grid=(N // tm,),
        in_specs=[
            pl.BlockSpec((tm, D), lambda i: (i, 0)),
            pl.BlockSpec((E, D), lambda i: (0, 0)),
            pl.BlockSpec((E, LANES), lambda i: (0, 0)),
            pl.BlockSpec((tm, tm), lambda i: (0, 0)),
        ],
        out_specs=[
            pl.BlockSpec((TOP_K, tm), lambda i: (0, i)),
            pl.BlockSpec((TOP_K, tm), lambda i: (0, i)),
            pl.BlockSpec((TOP_K, tm), lambda i: (0, i)),
            pl.BlockSpec((E, LANES), lambda i: (0, 0)),
        ],
        out_shape=[jax.ShapeDtypeStruct((TOP_K, N), I32), jax.ShapeDtypeStruct((TOP_K, N), F32),
                   jax.ShapeDtypeStruct((TOP_K, N), I32), jax.ShapeDtypeStruct((E, LANES), F32)],
        scratch_shapes=[pltpu.VMEM((E, LANES), F32)],
        compiler_params=_cparams(("arbitrary",), 32),
        name="router",
    )(h2, router_wt, router_b, upper)


MOE_TM = 256


def _dispatch_kernel(dest_ref, src_ref, xs_in_ref, xs_ref, sem, *, tb):
    del xs_in_ref
    t0 = pl.program_id(0) * tb

    def issue(r, c):
        for k in range(TOP_K):
            d = dest_ref[0, k * tb + r]
            pltpu.make_async_copy(src_ref.at[pl.ds(t0 + r, 1)], xs_ref.at[pl.ds(d, 1)], sem).start()
        return c

    lax.fori_loop(0, tb, issue, 0, unroll=2)

    def drain(r, c):
        for k in range(TOP_K):
            pltpu.make_async_copy(src_ref.at[pl.ds(0, 1)], xs_ref.at[pl.ds(0, 1)], sem).wait()
        return c

    lax.fori_loop(0, tb, drain, 0, unroll=2)


def _block_major(dest, tb):
    K, N = dest.shape
    return dest.reshape(K, N // tb, tb).transpose(1, 0, 2).reshape(N // tb, 1, K * tb)


def dispatch(src, dest, n_slots):
    N, W = src.shape
    tb = 512 if N % 512 == 0 else 256
    kern = functools.partial(_dispatch_kernel, tb=tb)
    return pl.pallas_call(
        kern,
        grid=(N // tb,),
        in_specs=[
            pl.BlockSpec((None, 1, TOP_K * tb), lambda i: (i, 0, 0), memory_space=pltpu.SMEM),
            pl.BlockSpec(memory_space=pl.ANY),
            pl.BlockSpec(memory_space=pl.ANY),
        ],
        out_specs=pl.BlockSpec(memory_space=pl.ANY),
        out_shape=jax.ShapeDtypeStruct((n_slots, W), src.dtype),
        scratch_shapes=[pltpu.SemaphoreType.DMA(())],
        input_output_aliases={2: 0},
        compiler_params=_cparams(("arbitrary",), 16),
        name="dispatch",
    )(_block_major(dest, tb), src, jnp.zeros((n_slots, W), src.dtype))


def _expert_up_kernel(be_ref, nu_ref, xs_ref, wg_ref, wu_ref, h_ref, wgb_ref, wub_ref):
    i = pl.program_id(1)
    nu = nu_ref[0]
    half = xs_ref.shape[1]
    prev = be_ref[jnp.maximum(i - 1, 0)]
    fresh = (i == 0) | (be_ref[i] != prev)

    @pl.when((i < nu) & fresh)
    def _():
        wgb_ref[...] = wg_ref[...].astype(BF16)
        wub_ref[...] = wu_ref[...].astype(BF16)

    @pl.when(i < nu)
    def _():
        u = xs_ref[...]
        lo = pltpu.bitcast(u << 16, F32).astype(BF16)
        hi = pltpu.bitcast(u & jnp.uint32(0xFFFF0000), F32).astype(BF16)
        g = _dot(lo, wgb_ref[0:half, :]) + _dot(hi, wgb_ref[half:2 * half, :])
        up = _dot(lo, wub_ref[0:half, :]) + _dot(hi, wub_ref[half:2 * half, :])
        h_ref[...] = (_silu(g) * up).astype(h_ref.dtype)

    @pl.when(i >= nu)
    def _():
        h_ref[...] = jnp.zeros_like(h_ref)


def expert_up(xs, blk_e, n_used, w_gate, w_up, layer):
    S, half = xs.shape
    _, E, D, Fd = w_gate.shape
    tm, tn = MOE_TM, 512
    nb = S // tm
    row = lambda j, i, be, nu: jnp.minimum(i, nu[0] - 1)
    wspec = pl.BlockSpec((None, None, D, tn), lambda j, i, be, nu: (layer, be[row(j, i, be, nu)], 0, j))
    return pl.pallas_call(
        _expert_up_kernel,
        grid_spec=pltpu.PrefetchScalarGridSpec(
            num_scalar_prefetch=2,
            grid=(Fd // tn, nb),
            in_specs=[pl.BlockSpec((tm, half), lambda j, i, be, nu: (row(j, i, be, nu), 0)), wspec, wspec],
            out_specs=pl.BlockSpec((tm, tn), lambda j, i, be, nu: (i, j)),
            scratch_shapes=[pltpu.VMEM((D, tn), BF16), pltpu.VMEM((D, tn), BF16)],
        ),
        out_shape=jax.ShapeDtypeStruct((S, Fd), BF16),
        compiler_params=_cparams(("arbitrary", "arbitrary"), 56),
        name="expert_up",
    )(blk_e, n_used, xs, w_gate, w_up)


def _expert_down_kernel(be_ref, nu_ref, h_ref, wd_ref, y_ref, wdb_ref):
    i = pl.program_id(0)
    nu = nu_ref[0]
    prev = be_ref[jnp.maximum(i - 1, 0)]
    fresh = (i == 0) | (be_ref[i] != prev)

    @pl.when((i < nu) & fresh)
    def _():
        wdb_ref[...] = wd_ref[...].astype(BF16)

    @pl.when(i < nu)
    def _():
        y_ref[...] = _dot(h_ref[...], wdb_ref[...])

    @pl.when(i >= nu)
    def _():
        y_ref[...] = jnp.zeros_like(y_ref)


def expert_down(h, blk_e, n_used, w_down, layer):
    S, Fd = h.shape
    _, E, _, D = w_down.shape
    tm = MOE_TM
    nb = S // tm
    row = lambda i, be, nu: jnp.minimum(i, nu[0] - 1)
    return pl.pallas_call(
        _expert_down_kernel,
        grid_spec=pltpu.PrefetchScalarGridSpec(
            num_scalar_prefetch=2,
            grid=(nb,),
            in_specs=[
                pl.BlockSpec((tm, Fd), lambda i, be, nu: (row(i, be, nu), 0)),
                pl.BlockSpec((None, None, Fd, D), lambda i, be, nu: (layer, be[row(i, be, nu)], 0, 0)),
            ],
            out_specs=pl.BlockSpec((tm, D), lambda i, be, nu: (i, 0)),
            scratch_shapes=[pltpu.VMEM((Fd, D), BF16)],
        ),
        out_shape=jax.ShapeDtypeStruct((S, D), F32),
        compiler_params=_cparams(("arbitrary",), 56),
        name="expert_down",
    )(blk_e, n_used, h, w_down)


def _combine_kernel(dcur_ref, dnext_ref, y_ref, w_ref, sh_ref, x1_ref, g_ref, mod_ref, o_ref, buf_ref, sem, *, tb, nb):
    i = pl.program_id(0)
    cur = lax.rem(i, 2)

    def fetch(dref, half):
        def issue(r, c):
            for k in range(TOP_K):
                d = dref[0, k * tb + r]
                pltpu.make_async_copy(y_ref.at[pl.ds(d, 1)], buf_ref.at[half, k, pl.ds(r, 1)], sem.at[half]).start()
            return c

        lax.fori_loop(0, tb, issue, 0, unroll=2)

    @pl.when(i == 0)
    def _():
        fetch(dcur_ref, 0)

    @pl.when(i + 1 < nb)
    def _():
        fetch(dnext_ref, 1 - cur)

    def drain(r, c):
        for k in range(TOP_K):
            pltpu.make_async_copy(y_ref.at[pl.ds(0, 1)], buf_ref.at[cur, k, pl.ds(0, 1)], sem.at[cur]).wait()
        return c

    lax.fori_loop(0, tb, drain, 0, unroll=2)
    w = w_ref[...]
    acc = sh_ref[...]
    for k in range(TOP_K):
        acc = acc + w[:, k:k + 1] * buf_ref[cur, k]
    o_ref[...] = x1_ref[...] + mod_ref[5:6, :] * (_rms(acc) * g_ref[3:4, :])


def combine(y_sorted, dest, w, shared, x1, g, mods, n_ctx, t_lat):
    N, D = x1.shape
    tb = 64
    nb = N // tb
    dest3 = _block_major(dest, tb)
    kern = functools.partial(_combine_kernel, tb=tb, nb=nb)
    return pl.pallas_call(
        kern,
        grid=(nb,),
        in_specs=[
            pl.BlockSpec((None, 1, TOP_K * tb), lambda i: (i, 0, 0), memory_space=pltpu.SMEM),
            pl.BlockSpec((None, 1, TOP_K * tb), lambda i: (jnp.minimum(i + 1, nb - 1), 0, 0), memory_space=pltpu.SMEM),
            pl.BlockSpec(memory_space=pl.ANY),
            pl.BlockSpec((tb, TOP_K), lambda i: (i, 0)),
            pl.BlockSpec((tb, D), lambda i: (i, 0)),
            pl.BlockSpec((tb, D), lambda i: (i, 0)),
            pl.BlockSpec((4, D), lambda i: (0, 0)),
            pl.BlockSpec((None, 6, D), lambda i: (_mod_row_index(i, tb, n_ctx, t_lat), 0, 0)),
        ],
        out_specs=pl.BlockSpec((tb, D), lambda i: (i, 0)),
        out_shape=jax.ShapeDtypeStruct((N, D), F32),
        scratch_shapes=[pltpu.VMEM((2, TOP_K, tb, D), F32), pltpu.SemaphoreType.DMA((2,))],
        compiler_params=_cparams(("arbitrary",), 40),
        name="combine",
    )(dest3, dest3, y_sorted, w, shared, x1, g, mods)


def moe_ffn(h2, hp, x1, p, g, mods, n_ctx, t_lat, layer):
    N, D = x1.shape
    tm = MOE_TM
    idx, w, rank, cnt = router(h2, p['router_wt'], p['router_b'])
    counts = cnt[:, 0].astype(I32)
    padded = (counts + tm - 1) // tm * tm
    pend = jnp.cumsum(padded)
    pstart = pend - padded
    n_blocks = (N * TOP_K + N_EXPERTS * (tm - 1) + tm - 1) // tm
    n_used = (pend[-1] // tm).astype(I32).reshape(1)
    blk_start = jnp.arange(n_blocks, dtype=I32) * tm
    blk_e = jnp.minimum(jnp.sum((pend[None, :] <= blk_start[:, None]).astype(I32), axis=1), N_EXPERTS - 1)
    hit = idx[:, :, None] == jnp.arange(N_EXPERTS, dtype=I32)[None, None, :]
    dest = jnp.sum(jnp.where(hit, pstart[None, None, :], 0), axis=-1) + rank
    xs = dispatch(hp, dest, n_blocks * tm)
    hid = expert_up(xs, blk_e, n_used, p['exp_w_gate'], p['exp_w_up'], layer)
    ys = expert_down(hid, blk_e, n_used, p['exp_w_down'], layer)
    sh_hid = swiglu(h2, p['sh_w_gate'], p['sh_w_up'])
    shared = matmul(sh_hid, p['sh_w_down'])
    return combine(ys, dest, w.T, shared, x1, g, mods, n_ctx, t_lat)


def _axial_rope_tables(n):
    rows = n // GRID_W
    t = jnp.arange(rows * GRID_W)
    row = (t // GRID_W).astype(F32)
    col = (t % GRID_W).astype(F32)
    n_freq = LANES // 4
    inv = ROPE_BASE ** (-jnp.arange(n_freq, dtype=F32) / n_freq)
    ang = jnp.concatenate([row[:, None] * inv, col[:, None] * inv], axis=-1)
    cos, sin = jnp.cos(ang), jnp.sin(ang)
    return jnp.concatenate([cos, cos], axis=-1), jnp.concatenate([-sin, sin], axis=-1)


def _layer_params(l, w_in, conv_w, conv_b, lru_wa, lru_ba, lru_wx, lru_bx, lru_lam, diff_lam, diff_subln,
                  gla_w, gla_b, gla_norm, w_out, router_w, router_b, exp_w_gate, exp_w_up, exp_w_down,
                  sh_w_gate, sh_w_up, sh_w_down):
    d_lru = conv_w.shape[-1]
    n_main = w_in.shape[-1] - 2 * GLA_RANK
    dk = LANES
    wa, wx = lru_wa[l], lru_wx[l]
    w_gate = jnp.concatenate([wa[0], wx[0], wa[1], wx[1]], axis=-1).astype(BF16)
    ba = lru_ba[l].reshape(2, H_LRU, 1, -1)
    bx = lru_bx[l].reshape(2, H_LRU, 1, -1)
    b_gate = jnp.concatenate([ba[0], bx[0], ba[1], bx[1]], axis=-1)
    gw = gla_w[l]
    w_glog = jnp.zeros((H_GLA, LANES, 2 * dk), F32)
    for d in range(2):
        blk = gw[d].reshape(GLA_RANK, H_GLA, dk).transpose(1, 0, 2)
        w_glog = w_glog.at[:, d * GLA_RANK:(d + 1) * GLA_RANK, d * dk:(d + 1) * dk].set(blk)
    gb = gla_b[l].reshape(2, H_GLA, 1, dk)
    b_glog = jnp.concatenate([gb[0], gb[1]], axis=-1)
    w_lr = jnp.zeros((w_in.shape[1], LANES), F32).at[:, :2 * GLA_RANK].set(w_in[l][:, n_main:])
    return {
        'w_main': w_in[l][:, :n_main].astype(BF16), 'w_lr': w_lr.astype(BF16),
        'conv_w': conv_w[l], 'conv_b': conv_b[l].reshape(1, d_lru),
        'w_gate': w_gate, 'b_gate': b_gate, 'lru_lam': lru_lam[l],
        'diff_lam': diff_lam[l], 'diff_subln': diff_subln[l].reshape(1, -1),
        'w_glog': w_glog.astype(BF16), 'b_glog': b_glog, 'gla_norm': gla_norm[l].reshape(1, -1),
        'w_out': w_out[l].astype(BF16),
        'router_wt': router_w[l].T.astype(BF16),
        'router_b': jnp.broadcast_to(router_b[l][:, None], (N_EXPERTS, LANES)),
        'exp_w_gate': exp_w_gate, 'exp_w_up': exp_w_up, 'exp_w_down': exp_w_down,
        'sh_w_gate': sh_w_gate[l].astype(BF16), 'sh_w_up': sh_w_up[l].astype(BF16),
        'sh_w_down': sh_w_down[l].astype(BF16),
    }


def kernel(x_prompt, x_sample, cache_k, cache_v, state_lru, state_gla, c, c_ctx, mod_w, mod_b, norm_g, w_in,
           conv_w, conv_b, lru_wa, lru_ba, lru_wx, lru_bx, lru_lam, diff_lam, diff_subln, gla_w, gla_b,
           gla_norm, w_out, router_w, router_b, exp_w_gate, exp_w_up, exp_w_down, sh_w_gate, sh_w_up,
           sh_w_down):
    Bc, Tc, D = x_prompt.shape
    Bl, Tl, _ = x_sample.shape
    L = mod_w.shape[0]
    n_ctx, n_lat = Bc * Tc, Bl * Tl
    d_lru = conv_w.shape[-1]
    d_diff = H_DIFF * 2 * LANES
    d_gk = H_GLA * LANES
    d_gv = H_GLA * 2 * LANES
    col_qb = 2 * d_lru
    col_qc = col_qb + 3 * d_diff
    col_kc = col_qc + d_gk
    col_vc = col_kc + d_gk
    col_gc = col_vc + d_gv

    x = jnp.concatenate([x_prompt.reshape(n_ctx, D), x_sample.reshape(n_lat, D)], axis=0)
    n_rows = 16 * ((1 + Bl + 15) // 16)
    c_rows = jnp.zeros((n_rows, D), F32).at[0].set(c_ctx).at[1:1 + Bl].set(c)
    mods_all = modulation(c_rows, mod_w, mod_b).reshape(L, n_rows, 6, D)
    rope = _axial_rope_tables(Tl)
    cache_k2 = cache_k.reshape(*cache_k.shape[:3], -1)
    cache_v2 = cache_v.reshape(*cache_v.shape[:3], -1)
    zeros_lru = jnp.zeros((Bc, 2, d_lru), F32)
    zeros_gla = jnp.zeros((Bc, 2, H_GLA, LANES, 2 * LANES), F32)

    ks, vs, lrus, glas = [], [], [], []
    for l in range(L):
        p = _layer_params(l, w_in, conv_w, conv_b, lru_wa, lru_ba, lru_wx, lru_bx, lru_lam, diff_lam,
                          diff_subln, gla_w, gla_b, gla_norm, w_out, router_w, router_b, exp_w_gate,
                          exp_w_up, exp_w_down, sh_w_gate, sh_w_up, sh_w_down)
        lam_init = 0.8 - 0.6 * math.exp(-0.3 * l)
        mods = mods_all[l]
        g = norm_g[l]
        h = norm_mod(x, g[0:1], mods, n_ctx, Tl)
        z = matmul(h, p['w_main'])
        z_lr = matmul(h, p['w_lr'])
        ycat = jnp.zeros((n_ctx + n_lat, D), BF16)
        ycat, lru_c = lru_mixer(z, ycat, 0, Bc, Tc, p['conv_w'], p['conv_b'], p['w_gate'], p['b_gate'],
                                p['lru_lam'], zeros_lru)
        ycat, _ = lru_mixer(z, ycat, n_ctx, Bl, Tl, p['conv_w'], p['conv_b'], p['w_gate'], p['b_gate'],
                            p['lru_lam'], state_lru[:, l])
        ycat = diff_attention(z, ycat, d_lru, 0, Bc, Tc, col_qb, None, None, l, None, p['diff_lam'],
                              p['diff_subln'], lam_init)
        ycat = diff_attention(z, ycat, d_lru, n_ctx, Bl, Tl, col_qb, cache_k2, cache_v2, l, rope, p['diff_lam'],
                              p['diff_subln'], lam_init)
        ycat, gla_c = gla_mixer(z, z_lr, ycat, d_lru + d_diff, 0, Bc, Tc, col_qc, col_kc, col_vc, col_gc,
                                p['w_glog'], p['b_glog'], p['gla_norm'], zeros_gla)
        ycat, _ = gla_mixer(z, z_lr, ycat, d_lru + d_diff, n_ctx, Bl, Tl, col_qc, col_kc, col_vc, col_gc,
                            p['w_glog'], p['b_glog'], p['gla_norm'], state_gla[:, l])
        y = matmul(ycat, p['w_out'])
        x1, h2, hp = post_mix(x, y, g, mods, n_ctx, Tl)
        x = moe_ffn(h2, hp, x1, p, g, mods, n_ctx, Tl, l)
        ks.append(z[:n_ctx, col_qb + d_diff:col_qb + 2 * d_diff].reshape(Bc, Tc, H_DIFF, 2 * LANES))
        vs.append(z[:n_ctx, col_qb + 2 * d_diff:col_qb + 3 * d_diff].reshape(Bc, Tc, H_DIFF, 2 * LANES))
        lrus.append(lru_c)
        glas.append(gla_c)
    yp = x[:n_ctx].reshape(Bc, Tc, D)
    ys = x[n_ctx:].reshape(Bl, Tl, D)
    return (yp, ys, jnp.stack(ks, axis=1), jnp.stack(vs, axis=1), jnp.stack(lrus, axis=1), jnp.stack(glas, axis=1))
```

```python
import functools
import math

import jax
import jax.numpy as jnp
from jax import lax
from jax.experimental import pallas as pl
from jax.experimental.pallas import tpu as pltpu

F32 = jnp.float32
BF16 = jnp.bfloat16
U32 = jnp.uint32
I32 = jnp.int32

EPS = 1e-6
GRID_W = 64
H_LRU = 8
CONV_W = 4
CONV_LEFT = 2
LRU_C = 8.0
H_DIFF = 8
ROPE_BASE = 10000.0
H_GLA = 4
GLA_RANK = 16
GLA_TAU = 16.0
GLA_CHUNK = 64
N_EXPERTS = 64
TOP_K = 8
N_GROUPS = 8
TOPK_GROUPS = 4
ROUTED_SCALE = 2.5

V7X_VMEM_BYTES = 64 * 1024 * 1024
LANES = 128
SUBLANES = 8
MIB = 1024 * 1024


def _cparams(semantics, vmem_mib):
    assert vmem_mib * MIB < V7X_VMEM_BYTES
    return pltpu.CompilerParams(dimension_semantics=semantics, vmem_limit_bytes=vmem_mib * MIB)


def _rms(x):
    return x * lax.rsqrt(jnp.mean(x * x, axis=-1, keepdims=True) + EPS)


def _sigmoid(x):
    return 1.0 / (1.0 + jnp.exp(-x))


def _silu(x):
    return x * _sigmoid(x)


def _dot(a, b):
    return jnp.dot(a, b, preferred_element_type=F32)


def _dot_nt(a, b):
    return lax.dot_general(a, b, (((1,), (1,)), ((), ())), preferred_element_type=F32)


def _dot_tn(a, b):
    return lax.dot_general(a, b, (((0,), (0,)), ((), ())), preferred_element_type=F32)


def _mod_kernel(c_ref, w_ref, b_ref, o_ref):
    s = _silu(c_ref[...]).astype(BF16)
    o_ref[...] = _dot(s, w_ref[...].astype(BF16)) + b_ref[...]


def modulation(c_rows, mod_w, mod_b):
    L, D, N6 = mod_w.shape
    R = c_rows.shape[0]
    tn = 512
    return pl.pallas_call(
        _mod_kernel,
        grid=(L, N6 // tn),
        in_specs=[
            pl.BlockSpec((R, D), lambda l, j: (0, 0)),
            pl.BlockSpec((None, D, tn), lambda l, j: (l, 0, j)),
            pl.BlockSpec((None, 1, tn), lambda l, j: (l, 0, j)),
        ],
        out_specs=pl.BlockSpec((None, R, tn), lambda l, j: (l, 0, j)),
        out_shape=jax.ShapeDtypeStruct((L, R, N6), F32),
        compiler_params=_cparams(("arbitrary", "arbitrary"), 40),
        name="modulation",
    )(c_rows, mod_w, mod_b.reshape(L, 1, N6))


def _mod_row_index(i, tm, n_ctx, t_lat):
    r0 = i * tm
    return jnp.where(r0 < n_ctx, 0, 1 + (r0 - n_ctx) // t_lat)


def _norm_mod_kernel(x_ref, g_ref, mod_ref, o_ref):
    y = _rms(x_ref[...]) * g_ref[...]
    o_ref[...] = (y * (1.0 + mod_ref[1:2, :]) + mod_ref[0:1, :]).astype(o_ref.dtype)


def norm_mod(x, g, mods, n_ctx, t_lat):
    N, D = x.shape
    tm = 256
    return pl.pallas_call(
        _norm_mod_kernel,
        grid=(N // tm,),
        in_specs=[
            pl.BlockSpec((tm, D), lambda i: (i, 0)),
            pl.BlockSpec((1, D), lambda i: (0, 0)),
            pl.BlockSpec((None, 6, D), lambda i: (_mod_row_index(i, tm, n_ctx, t_lat), 0, 0)),
        ],
        out_specs=pl.BlockSpec((tm, D), lambda i: (i, 0)),
        out_shape=jax.ShapeDtypeStruct((N, D), BF16),
        compiler_params=_cparams(("arbitrary",), 32),
        name="norm_mod",
    )(x, g, mods)


def _mm_kernel(x_ref, w_ref, o_ref):
    o_ref[...] = _dot(x_ref[...], w_ref[...]).astype(o_ref.dtype)


def matmul(x, w, out_dtype=F32, tm=1024, tn=512):
    M, K = x.shape
    N = w.shape[1]
    tm = min(tm, M)
    tn = min(tn, N)
    assert M % tm == 0 and N % tn == 0
    return pl.pallas_call(
        _mm_kernel,
        grid=(M // tm, N // tn),
        in_specs=[pl.BlockSpec((tm, K), lambda i, j: (i, 0)), pl.BlockSpec((K, tn), lambda i, j: (0, j))],
        out_specs=pl.BlockSpec((tm, tn), lambda i, j: (i, j)),
        out_shape=jax.ShapeDtypeStruct((M, N), out_dtype),
        compiler_params=_cparams(("arbitrary", "arbitrary"), 48),
        name="matmul",
    )(x, w)


def _swiglu_kernel(x_ref, wg_ref, wu_ref, o_ref):
    x = x_ref[...]
    o_ref[...] = (_silu(_dot(x, wg_ref[...])) * _dot(x, wu_ref[...])).astype(o_ref.dtype)


def swiglu(x, wg, wu, tm=1024, tn=256):
    M, K = x.shape
    N = wg.shape[1]
    tm = min(tm, M)
    return pl.pallas_call(
        _swiglu_kernel,
        grid=(M // tm, N // tn),
        in_specs=[
            pl.BlockSpec((tm, K), lambda i, j: (i, 0)),
            pl.BlockSpec((K, tn), lambda i, j: (0, j)),
            pl.BlockSpec((K, tn), lambda i, j: (0, j)),
        ],
        out_specs=pl.BlockSpec((tm, tn), lambda i, j: (i, j)),
        out_shape=jax.ShapeDtypeStruct((M, N), BF16),
        compiler_params=_cparams(("arbitrary", "arbitrary"), 48),
        name="swiglu",
    )(x, wg, wu)


def _lru_kernel(xa_ref, ga_ref, cw_ref, cb_ref, wg_ref, bg_ref, lam_ref, h0_ref, ybuf_ref, y_ref, hl_ref,
                pad_ref, af_ref, uf_ref, ab_ref, ub_ref, *, T, CW):
    del ybuf_ref
    pad_ref[0:SUBLANES, :] = jnp.zeros((SUBLANES, CW), F32)
    pad_ref[SUBLANES + T:2 * SUBLANES + T, :] = jnp.zeros((SUBLANES, CW), F32)
    pad_ref[SUBLANES:SUBLANES + T, :] = xa_ref[...]
    xc = cb_ref[...] + jnp.zeros((T, CW), F32)
    for j in range(CONV_W):
        o = SUBLANES - CONV_LEFT + j
        xc = xc + cw_ref[j:j + 1, :] * pad_ref[o:o + T, :]
    lam = lam_ref[...]
    nl = -lam
    softplus = jnp.maximum(nl, 0.0) + jnp.log(1.0 + jnp.exp(-jnp.abs(nl)))
    bs = LANES
    for hh in range(CW // bs):
        cs = slice(hh * bs, (hh + 1) * bs)
        xb = xc[:, cs]
        gts = _dot(xb.astype(BF16), wg_ref[hh]) + bg_ref[hh]
        for d, (a_ref, u_ref) in enumerate(((af_ref, uf_ref), (ab_ref, ub_ref))):
            r = _sigmoid(gts[:, (2 * d) * bs:(2 * d + 1) * bs])
            ig = _sigmoid(gts[:, (2 * d + 1) * bs:(2 * d + 2) * bs])
            log_a = (-LRU_C) * r * softplus[d:d + 1, cs]
            a = jnp.exp(log_a)
            a_ref[:, cs] = a
            u_ref[:, cs] = jnp.sqrt(1.0 - a * a) * (ig * xb)

    def step(t, carry):
        hf, hb = carry
        hf = af_ref[pl.ds(t, 1), :] * hf + uf_ref[pl.ds(t, 1), :]
        uf_ref[pl.ds(t, 1), :] = hf
        tb = T - 1 - t
        hb = ab_ref[pl.ds(tb, 1), :] * hb + ub_ref[pl.ds(tb, 1), :]
        ub_ref[pl.ds(tb, 1), :] = hb
        return hf, hb

    hf, hb = lax.fori_loop(0, T, step, (h0_ref[0:1, :], h0_ref[1:2, :]), unroll=8)
    hl_ref[0:1, :] = hf
    hl_ref[1:2, :] = hb
    y_ref[...] = ((uf_ref[...] + ub_ref[...]) * jax.nn.gelu(ga_ref[...], approximate=True)).astype(y_ref.dtype)


def lru_mixer(z, ybuf, row0, B, T, conv_w, conv_b, w_gate, b_gate, lam, h0):
    d_lru = conv_w.shape[1]
    CW = 512
    nj = d_lru // CW
    rb0 = row0 // T
    kern = functools.partial(_lru_kernel, T=T, CW=CW)
    return pl.pallas_call(
        kern,
        grid=(B, nj),
        in_specs=[
            pl.BlockSpec((T, CW), lambda b, j: (rb0 + b, j)),
            pl.BlockSpec((T, CW), lambda b, j: (rb0 + b, nj + j)),
            pl.BlockSpec((CONV_W, CW), lambda b, j: (0, j)),
            pl.BlockSpec((1, CW), lambda b, j: (0, j)),
            pl.BlockSpec((CW // LANES, LANES, 4 * LANES), lambda b, j: (j, 0, 0)),
            pl.BlockSpec((CW // LANES, 1, 4 * LANES), lambda b, j: (j, 0, 0)),
            pl.BlockSpec((2, CW), lambda b, j: (0, j)),
            pl.BlockSpec((None, 2, CW), lambda b, j: (b, 0, j)),
            pl.BlockSpec(memory_space=pl.ANY),
        ],
        out_specs=[
            pl.BlockSpec((T, CW), lambda b, j: (rb0 + b, j)),
            pl.BlockSpec((None, 2, CW), lambda b, j: (b, 0, j)),
        ],
        out_shape=[jax.ShapeDtypeStruct(ybuf.shape, ybuf.dtype), jax.ShapeDtypeStruct((B, 2, d_lru), F32)],
        scratch_shapes=[pltpu.VMEM((T + 2 * SUBLANES, CW), F32)] + [pltpu.VMEM((T, CW), F32)] * 4,
        input_output_aliases={8: 0},
        compiler_params=_cparams(("arbitrary", "arbitrary"), 40),
        name="lru_mixer",
    )(z, z, conv_w, conv_b, w_gate, b_gate, lam, h0, ybuf)


def _rope(x, cos2, sin2):
    return x * cos2 + pltpu.roll(x, LANES // 2, axis=1) * sin2


def _attn_kernel(*refs, T, P, tq, use_rope, lam_init):
    it = iter(refs)
    q_ref, k_ref, v_ref = next(it), next(it), next(it)
    ck_ref = cv_ref = cos_ref = sin_ref = None
    if P:
        ck_ref, cv_ref = next(it), next(it)
    if use_rope:
        cos_ref, sin_ref = next(it), next(it)
    dl_ref, sub_ref, _, o_ref, kall_ref, vall_ref = next(it), next(it), next(it), next(it), next(it), next(it)
    dh = LANES
    qi = pl.program_id(2)

    @pl.when(qi == 0)
    def _():
        if P:
            kall_ref[0:P, :] = ck_ref[...].astype(BF16)
            vall_ref[0:P, :] = cv_ref[...].astype(BF16)
        for m in range(2):
            km = k_ref[:, m * dh:(m + 1) * dh]
            if use_rope:
                km = _rope(km, cos_ref[...], sin_ref[...])
            kall_ref[P:P + T, m * dh:(m + 1) * dh] = km.astype(BF16)
        vall_ref[P:P + T, :] = v_ref[...].astype(BF16)

    dl = dl_ref[...]
    lam = (jnp.exp(jnp.sum(dl[0:1, :] * dl[1:2, :], axis=-1, keepdims=True))
           - jnp.exp(jnp.sum(dl[2:3, :] * dl[3:4, :], axis=-1, keepdims=True)) + lam_init)
    qscale = (dh ** -0.5) * math.log2(math.e)
    pv, invs = [], []
    for m in range(2):
        qm = q_ref[:, m * dh:(m + 1) * dh]
        if use_rope:
            r0 = pl.multiple_of(qi * tq, tq)
            qm = _rope(qm, cos_ref[pl.ds(r0, tq), :], sin_ref[pl.ds(r0, tq), :])
        s2 = _dot_nt((qm * qscale).astype(BF16), kall_ref[:, m * dh:(m + 1) * dh])
        e = jnp.exp2(s2 - jnp.max(s2, axis=-1, keepdims=True))
        invs.append(1.0 / jnp.sum(e, axis=-1, keepdims=True))
        pv.append(_dot(e.astype(BF16), vall_ref[...]))
    o = pv[0] * invs[0] - pv[1] * (lam * invs[1])
    o_ref[...] = (_rms(o) * sub_ref[...] * (1.0 - lam_init)).astype(o_ref.dtype)


def diff_attention(z, ybuf, col_y, row0, B, T, col_q, cache_k, cache_v, layer, rope, diff_lam, subln, lam_init):
    H = H_DIFF
    dw = 2 * LANES
    P = 0 if cache_k is None else cache_k.shape[2]
    tq = min(T, 256)
    nq = T // tq
    rbq = row0 // tq
    rbk = row0 // T
    cq = col_q // dw
    kern = functools.partial(_attn_kernel, T=T, P=P, tq=tq, use_rope=rope is not None, lam_init=lam_init)
    in_specs = [
        pl.BlockSpec((tq, dw), lambda b, h, i: (rbq + b * nq + i, cq + h)),
        pl.BlockSpec((T, dw), lambda b, h, i: (rbk + b, cq + H + h)),
        pl.BlockSpec((T, dw), lambda b, h, i: (rbk + b, cq + 2 * H + h)),
    ]
    args = [z, z, z]
    if P:
        in_specs += [pl.BlockSpec((None, None, P, dw), lambda b, h, i: (b, layer, 0, h))] * 2
        args += [cache_k, cache_v]
    if rope is not None:
        in_specs += [pl.BlockSpec((T, LANES), lambda b, h, i: (0, 0))] * 2
        args += [rope[0], rope[1]]
    in_specs += [pl.BlockSpec((4, LANES), lambda b, h, i: (0, 0)), pl.BlockSpec((1, dw), lambda b, h, i: (0, 0)),
                 pl.BlockSpec(memory_space=pl.ANY)]
    args += [diff_lam, subln, ybuf]
    cy = col_y // dw
    return pl.pallas_call(
        kern,
        grid=(B, H, nq),
        in_specs=in_specs,
        out_specs=pl.BlockSpec((tq, dw), lambda b, h, i: (rbq + b * nq + i, cy + h)),
        out_shape=jax.ShapeDtypeStruct(ybuf.shape, ybuf.dtype),
        scratch_shapes=[pltpu.VMEM((P + T, dw), BF16), pltpu.VMEM((P + T, dw), BF16)],
        input_output_aliases={len(args) - 1: 0},
        compiler_params=_cparams(("arbitrary", "arbitrary", "arbitrary"), 40),
        name="diff_attention",
    )(*args)


def _gla_kernel(q_ref, k_ref, v_ref, gc_ref, lr_ref, wgl_ref, bgl_ref, gn_ref, s0_ref, ybuf_ref, y_ref, so_ref,
                glog_ref, sf_ref, sb_ref, of_ref, ob_ref, *, T):
    del ybuf_ref
    C = GLA_CHUNK
    dk = LANES
    nc = T // C
    gl = _dot(lr_ref[...].astype(BF16), wgl_ref[...]) + bgl_ref[...]
    glog_ref[...] = (jnp.minimum(gl, 0.0) - jnp.log(1.0 + jnp.exp(-jnp.abs(gl)))) * (1.0 / GLA_TAU)
    row = lax.broadcasted_iota(I32, (C, C), 0)
    col = lax.broadcasted_iota(I32, (C, C), 1)
    qscale = dk ** -0.5
    ones = jnp.ones((C, dk), BF16)
    sf_ref[...] = s0_ref[0]
    sb_ref[...] = s0_ref[1]

    def one_chunk(c, d, s_ref, o_ref):
        keep = (row >= col) if d == 0 else (row <= col)
        tri = jnp.where(keep, 1.0, 0.0).astype(BF16)
        r0 = pl.multiple_of(c * C, C)
        g = glog_ref[pl.ds(r0, C), d * dk:(d + 1) * dk]
        g_hi = g.astype(BF16)
        g_lo = (g - g_hi.astype(F32)).astype(BF16)
        b = _dot(tri, g_hi) + _dot(tri, g_lo)
        b_end = b[C - 1:C, :] if d == 0 else b[0:1, :]
        q = q_ref[pl.ds(r0, C), :] * qscale
        k = k_ref[pl.ds(r0, C), :]
        v = v_ref[pl.ds(r0, C), :].astype(BF16)
        qt = (q * jnp.exp(b)).astype(BF16)
        kt = (k * jnp.exp(-b)).astype(BF16)
        att = jnp.where(keep, _dot_nt(qt, kt), 0.0)
        S = s_ref[...]
        o_ref[pl.ds(r0, C), :] = _dot(att.astype(BF16), v) + _dot(qt, S.astype(BF16))
        kd = (k * jnp.exp(b_end - b)).astype(BF16)
        bcol = _dot_tn(g_hi, ones) + _dot_tn(g_lo, ones)
        ecol = jnp.exp(bcol)
        s_ref[...] = jnp.concatenate([ecol, ecol], axis=1) * S + _dot_tn(kd, v)

    def chunk(ci, carry):
        one_chunk(ci, 0, sf_ref, of_ref)
        one_chunk(nc - 1 - ci, 1, sb_ref, ob_ref)
        return carry

    lax.fori_loop(0, nc, chunk, 0)
    so_ref[0] = sf_ref[...]
    so_ref[1] = sb_ref[...]
    oc = of_ref[...] + ob_ref[...]
    y_ref[...] = (_rms(oc) * gn_ref[...] * _silu(gc_ref[...])).astype(y_ref.dtype)


def gla_mixer(z, z_lr, ybuf, col_y, row0, B, T, col_q, col_k, col_v, col_g, w_glog, b_glog, gnorm, s0):
    H = H_GLA
    dk, dv = LANES, 2 * LANES
    rb = row0 // T
    kern = functools.partial(_gla_kernel, T=T)
    return pl.pallas_call(
        kern,
        grid=(B, H),
        in_specs=[
            pl.BlockSpec((T, dk), lambda b, h: (rb + b, col_q // dk + h)),
            pl.BlockSpec((T, dk), lambda b, h: (rb + b, col_k // dk + h)),
            pl.BlockSpec((T, dv), lambda b, h: (rb + b, col_v // dv + h)),
            pl.BlockSpec((T, dv), lambda b, h: (rb + b, col_g // dv + h)),
            pl.BlockSpec((T, LANES), lambda b, h: (rb + b, 0)),
            pl.BlockSpec((None, LANES, 2 * dk), lambda b, h: (h, 0, 0)),
            pl.BlockSpec((None, 1, 2 * dk), lambda b, h: (h, 0, 0)),
            pl.BlockSpec((1, dv), lambda b, h: (0, 0)),
            pl.BlockSpec((None, 2, None, dk, dv), lambda b, h: (b, 0, h, 0, 0)),
            pl.BlockSpec(memory_space=pl.ANY),
        ],
        out_specs=[
            pl.BlockSpec((T, dv), lambda b, h: (rb + b, col_y // dv + h)),
            pl.BlockSpec((None, 2, None, dk, dv), lambda b, h: (b, 0, h, 0, 0)),
        ],
        out_shape=[jax.ShapeDtypeStruct(ybuf.shape, ybuf.dtype), jax.ShapeDtypeStruct((B, 2, H, dk, dv), F32)],
        scratch_shapes=[pltpu.VMEM((T, 2 * dk), F32), pltpu.VMEM((dk, dv), F32), pltpu.VMEM((dk, dv), F32),
                        pltpu.VMEM((T, dv), F32), pltpu.VMEM((T, dv), F32)],
        input_output_aliases={9: 0},
        compiler_params=_cparams(("arbitrary", "arbitrary"), 32),
        name="gla_mixer",
    )(z, z, z, z, z_lr, w_glog, b_glog, gnorm, s0, ybuf)


def _post_mix_kernel(x_ref, y_ref, g_ref, mod_ref, x1_ref, h2_ref, hp_ref):
    D = x_ref.shape[1]
    x1 = x_ref[...] + mod_ref[2:3, :] * (_rms(y_ref[...]) * g_ref[1:2, :])
    x1_ref[...] = x1
    h = (_rms(x1) * g_ref[2:3, :]) * (1.0 + mod_ref[4:5, :]) + mod_ref[3:4, :]
    hb = h.astype(BF16)
    h2_ref[...] = hb
    u = pltpu.bitcast(hb.astype(F32), U32)
    hp_ref[...] = (u[:, :D // 2] >> 16) | (u[:, D // 2:] & jnp.uint32(0xFFFF0000))


def post_mix(x, y, g, mods, n_ctx, t_lat):
    N, D = x.shape
    tm = 256
    return pl.pallas_call(
        _post_mix_kernel,
        grid=(N // tm,),
        in_specs=[
            pl.BlockSpec((tm, D), lambda i: (i, 0)),
            pl.BlockSpec((tm, D), lambda i: (i, 0)),
            pl.BlockSpec((4, D), lambda i: (0, 0)),
            pl.BlockSpec((None, 6, D), lambda i: (_mod_row_index(i, tm, n_ctx, t_lat), 0, 0)),
        ],
        out_specs=[
            pl.BlockSpec((tm, D), lambda i: (i, 0)),
            pl.BlockSpec((tm, D), lambda i: (i, 0)),
            pl.BlockSpec((tm, D // 2), lambda i: (i, 0)),
        ],
        out_shape=[jax.ShapeDtypeStruct((N, D), F32), jax.ShapeDtypeStruct((N, D), BF16),
                   jax.ShapeDtypeStruct((N, D // 2), U32)],
        compiler_params=_cparams(("arbitrary",), 48),
        name="post_mix",
    )(x, y, g, mods)


def _router_kernel(h_ref, rwt_ref, rb_ref, upper_ref, idx_ref, w_ref, rank_ref, cnt_ref, carry_ref, *, tm):
    i = pl.program_id(0)
    per = N_EXPERTS // N_GROUPS

    @pl.when(i == 0)
    def _():
        carry_ref[...] = jnp.zeros_like(carry_ref)

    logits = _dot_nt(rwt_ref[...], h_ref[...])
    s_all = _sigmoid(logits)
    sel_all = s_all + rb_ref[...][:, 0:1]
    sub = lax.broadcasted_iota(I32, (per, tm), 0)
    neg = -jnp.inf
    s_g, sel_g, gscore = [], [], []
    for g in range(N_GROUPS):
        blk = sel_all[g * per:(g + 1) * per, :]
        s_g.append(s_all[g * per:(g + 1) * per, :])
        sel_g.append(blk)
        m1 = jnp.max(blk, axis=0, keepdims=True)
        first = jnp.min(jnp.where(blk == m1, sub, per), axis=0, keepdims=True)
        m2 = jnp.max(jnp.where(sub == first, neg, blk), axis=0, keepdims=True)
        gscore.append(m1 + m2)
    for g in range(N_GROUPS):
        beat = jnp.zeros((1, tm), F32)
        for g2 in range(N_GROUPS):
            if g2 == g:
                continue
            wins = (gscore[g2] > gscore[g]) | ((gscore[g2] == gscore[g]) & (g2 < g))
            beat = beat + jnp.where(wins, 1.0, 0.0)
        sel_g[g] = jnp.where(beat < TOPK_GROUPS, sel_g[g], neg)
    eidx = [sub + g * per for g in range(N_GROUPS)]
    onehot = [jnp.zeros((per, tm), F32) for _ in range(N_GROUPS)]
    picks, wts = [], []
    for k in range(TOP_K):
        m = sel_g[0]
        for g in range(1, N_GROUPS):
            m = jnp.maximum(m, sel_g[g])
        m = jnp.max(m, axis=0, keepdims=True)
        cand = jnp.where(sel_g[0] == m, eidx[0], N_EXPERTS)
        for g in range(1, N_GROUPS):
            cand = jnp.minimum(cand, jnp.where(sel_g[g] == m, eidx[g], N_EXPERTS))
        pick = jnp.min(cand, axis=0, keepdims=True)
        wk = jnp.zeros((per, tm), F32)
        for g in range(N_GROUPS):
            hit = eidx[g] == pick
            wk = wk + jnp.where(hit, s_g[g], 0.0)
            onehot[g] = onehot[g] + jnp.where(hit, 1.0, 0.0)
            sel_g[g] = jnp.where(hit, neg, sel_g[g])
        picks.append(pick)
        wts.append(jnp.sum(wk, axis=0, keepdims=True))
    wsum = wts[0]
    for k in range(1, TOP_K):
        wsum = wsum + wts[k]
    oh = jnp.concatenate(onehot, axis=0)
    carry = carry_ref[...][:, 0:1]
    before = _dot(oh.astype(BF16), upper_ref[...]) + carry
    for k in range(TOP_K):
        idx_ref[k:k + 1, :] = picks[k]
        w_ref[k:k + 1, :] = wts[k] / wsum * ROUTED_SCALE
        rk = jnp.zeros((per, tm), F32)
        for g in range(N_GROUPS):
            rk = rk + jnp.where(eidx[g] == picks[k], before[g * per:(g + 1) * per, :], 0.0)
        rank_ref[k:k + 1, :] = jnp.sum(rk, axis=0, keepdims=True).astype(I32)
    carry_ref[...] = carry_ref[...] + jnp.sum(oh, axis=1, keepdims=True)
    cnt_ref[...] = carry_ref[...]


def router(h2, router_wt, router_b):
    N, D = h2.shape
    tm = 512 if N % 512 == 0 else 256
    E = N_EXPERTS
    upper = (jnp.arange(tm)[:, None] < jnp.arange(tm)[None, :]).astype(BF16)
    kern = functools.partial(_router_kernel, tm=tm)
    return pl.pallas_call(
        kern,
        grid=(N // tm,),
        in_specs=[
            pl.BlockSpec((tm, D), lambda i: (i, 0)),
            pl.BlockSpec((E, D), lambda i: (0, 0)),
            pl.BlockSpec((E, LANES), lambda i: (0, 0)),
            pl.BlockSpec((tm, tm), lambda i: (0, 0)),
        ],
        out_specs=[
            pl.BlockSpec((TOP_K, tm), lambda i: (0, i)),
            pl.BlockSpec((TOP_K, tm), lambda i: (0, i)),
            pl.BlockSpec((TOP_K, tm), lambda i: (0, i)),
            pl.BlockSpec((E, LANES), lambda i: (0, 0)),
        ],
        out_shape=[jax.ShapeDtypeStruct((TOP_K, N), I32), jax.ShapeDtypeStruct((TOP_K, N), F32),
                   jax.ShapeDtypeStruct((TOP_K, N), I32), jax.ShapeDtypeStruct((E, LANES), F32)],
        scratch_shapes=[pltpu.VMEM((E, LANES), F32)],
        compiler_params=_cparams(("arbitrary",), 32),
        name="router",
    )(h2, router_wt, router_b, upper)


MOE_TM = 256


def _dispatch_kernel(dest_ref, src_ref, xs_in_ref, xs_ref, sem, *, tb):
    del xs_in_ref

    def issue(r, c):
        for k in range(TOP_K):
            d = dest_ref[0, k * tb + r]
            pltpu.make_async_copy(src_ref.at[pl.ds(r, 1)], xs_ref.at[pl.ds(d, 1)], sem).start()
        return c

    lax.fori_loop(0, tb, issue, 0, unroll=2)

    def drain(r, c):
        for k in range(TOP_K):
            pltpu.make_async_copy(src_ref.at[pl.ds(0, 1)], xs_ref.at[pl.ds(0, 1)], sem).wait()
        return c

    lax.fori_loop(0, tb, drain, 0, unroll=2)


def _block_major(dest, tb):
    K, N = dest.shape
    return dest.reshape(K, N // tb, tb).transpose(1, 0, 2).reshape(N // tb, 1, K * tb)


def dispatch(src, dest, n_slots):
    N, W = src.shape
    tb = 256
    kern = functools.partial(_dispatch_kernel, tb=tb)
    return pl.pallas_call(
        kern,
        grid=(N // tb,),
        in_specs=[
            pl.BlockSpec((None, 1, TOP_K * tb), lambda i: (i, 0, 0), memory_space=pltpu.SMEM),
            pl.BlockSpec((tb, W), lambda i: (i, 0)),
            pl.BlockSpec(memory_space=pl.ANY),
        ],
        out_specs=pl.BlockSpec(memory_space=pl.ANY),
        out_shape=jax.ShapeDtypeStruct((n_slots, W), src.dtype),
        scratch_shapes=[pltpu.SemaphoreType.DMA(())],
        input_output_aliases={2: 0},
        compiler_params=_cparams(("arbitrary",), 16),
        name="dispatch",
    )(_block_major(dest, tb), src, jnp.zeros((n_slots, W), src.dtype))


def _expert_up_kernel(be_ref, nu_ref, xs_ref, wg_ref, wu_ref, h_ref, wgb_ref, wub_ref):
    i = pl.program_id(1)
    nu = nu_ref[0]
    half = xs_ref.shape[1]
    prev = be_ref[jnp.maximum(i - 1, 0)]
    fresh = (i == 0) | (be_ref[i] != prev)

    @pl.when((i < nu) & fresh)
    def _():
        wgb_ref[...] = wg_ref[...].astype(BF16)
        wub_ref[...] = wu_ref[...].astype(BF16)

    @pl.when(i < nu)
    def _():
        u = xs_ref[...]
        lo = pltpu.bitcast(u << 16, F32).astype(BF16)
        hi = pltpu.bitcast(u & jnp.uint32(0xFFFF0000), F32).astype(BF16)
        g = _dot(lo, wgb_ref[0:half, :]) + _dot(hi, wgb_ref[half:2 * half, :])
        up = _dot(lo, wub_ref[0:half, :]) + _dot(hi, wub_ref[half:2 * half, :])
        h_ref[...] = (_silu(g) * up).astype(h_ref.dtype)

    @pl.when(i >= nu)
    def _():
        h_ref[...] = jnp.zeros_like(h_ref)


def expert_up(xs, blk_e, n_used, w_gate, w_up, layer):
    S, half = xs.shape
    _, E, D, Fd = w_gate.shape
    tm, tn = MOE_TM, 512
    nb = S // tm
    row = lambda j, i, be, nu: jnp.minimum(i, nu[0] - 1)
    wspec = pl.BlockSpec((None, None, D, tn), lambda j, i, be, nu: (layer, be[row(j, i, be, nu)], 0, j))
    return pl.pallas_call(
        _expert_up_kernel,
        grid_spec=pltpu.PrefetchScalarGridSpec(
            num_scalar_prefetch=2,
            grid=(Fd // tn, nb),
            in_specs=[pl.BlockSpec((tm, half), lambda j, i, be, nu: (row(j, i, be, nu), 0)), wspec, wspec],
            out_specs=pl.BlockSpec((tm, tn), lambda j, i, be, nu: (i, j)),
            scratch_shapes=[pltpu.VMEM((D, tn), BF16), pltpu.VMEM((D, tn), BF16)],
        ),
        out_shape=jax.ShapeDtypeStruct((S, Fd), BF16),
        compiler_params=_cparams(("arbitrary", "arbitrary"), 56),
        name="expert_up",
    )(blk_e, n_used, xs, w_gate, w_up)


def _expert_down_kernel(be_ref, nu_ref, h_ref, wd_ref, y_ref, wdb_ref):
    i = pl.program_id(0)
    nu = nu_ref[0]
    prev = be_ref[jnp.maximum(i - 1, 0)]
    fresh = (i == 0) | (be_ref[i] != prev)

    @pl.when((i < nu) & fresh)
    def _():
        wdb_ref[...] = wd_ref[...].astype(BF16)

    @pl.when(i < nu)
    def _():
        y_ref[...] = _dot(h_ref[...], wdb_ref[...])

    @pl.when(i >= nu)
    def _():
        y_ref[...] = jnp.zeros_like(y_ref)


def expert_down(h, blk_e, n_used, w_down, layer):
    S, Fd = h.shape
    _, E, _, D = w_down.shape
    tm = MOE_TM
    nb = S // tm
    row = lambda i, be, nu: jnp.minimum(i, nu[0] - 1)
    return pl.pallas_call(
        _expert_down_kernel,
        grid_spec=pltpu.PrefetchScalarGridSpec(
            num_scalar_prefetch=2,
            grid=(nb,),
            in_specs=[
                pl.BlockSpec((tm, Fd), lambda i, be, nu: (row(i, be, nu), 0)),
                pl.BlockSpec((None, None, Fd, D), lambda i, be, nu: (layer, be[row(i, be, nu)], 0, 0)),
            ],
            out_specs=pl.BlockSpec((tm, D), lambda i, be, nu: (i, 0)),
            scratch_shapes=[pltpu.VMEM((Fd, D), BF16)],
        ),
        out_shape=jax.ShapeDtypeStruct((S, D), F32),
        compiler_params=_cparams(("arbitrary",), 56),
        name="expert_down",
    )(blk_e, n_used, h, w_down)


def _combine_kernel(dcur_ref, dnext_ref, y_ref, w_ref, sh_ref, x1_ref, g_ref, mod_ref, o_ref, buf_ref, sem, *, tb, nb):
    i = pl.program_id(0)
    cur = lax.rem(i, 2)

    def fetch(dref, half):
        def issue(r, c):
            for k in range(TOP_K):
                d = dref[0, k * tb + r]
                pltpu.make_async_copy(y_ref.at[pl.ds(d, 1)], buf_ref.at[half, k, pl.ds(r, 1)], sem.at[half]).start()
            return c

        lax.fori_loop(0, tb, issue, 0, unroll=2)

    @pl.when(i == 0)
    def _():
        fetch(dcur_ref, 0)

    @pl.when(i + 1 < nb)
    def _():
        fetch(dnext_ref, 1 - cur)

    def drain(r, c):
        for k in range(TOP_K):
            pltpu.make_async_copy(y_ref.at[pl.ds(0, 1)], buf_ref.at[cur, k, pl.ds(0, 1)], sem.at[cur]).wait()
        return c

    lax.fori_loop(0, tb, drain, 0, unroll=2)
    w = w_ref[...]
    acc = sh_ref[...]
    for k in range(TOP_K):
        acc = acc + w[:, k:k + 1] * buf_ref[cur, k]
    o_ref[...] = x1_ref[...] + mod_ref[5:6, :] * (_rms(acc) * g_ref[3:4, :])


def combine(y_sorted, dest, w, shared, x1, g, mods, n_ctx, t_lat):
    N, D = x1.shape
    tb = 64
    nb = N // tb
    dest3 = _block_major(dest, tb)
    kern = functools.partial(_combine_kernel, tb=tb, nb=nb)
    return pl.pallas_call(
        kern,
        grid=(nb,),
        in_specs=[
            pl.BlockSpec((None, 1, TOP_K * tb), lambda i: (i, 0, 0), memory_space=pltpu.SMEM),
            pl.BlockSpec((None, 1, TOP_K * tb), lambda i: (jnp.minimum(i + 1, nb - 1), 0, 0), memory_space=pltpu.SMEM),
            pl.BlockSpec(memory_space=pl.ANY),
            pl.BlockSpec((tb, TOP_K), lambda i: (i, 0)),
            pl.BlockSpec((tb, D), lambda i: (i, 0)),
            pl.BlockSpec((tb, D), lambda i: (i, 0)),
            pl.BlockSpec((4, D), lambda i: (0, 0)),
            pl.BlockSpec((None, 6, D), lambda i: (_mod_row_index(i, tb, n_ctx, t_lat), 0, 0)),
        ],
        out_specs=pl.BlockSpec((tb, D), lambda i: (i, 0)),
        out_shape=jax.ShapeDtypeStruct((N, D), F32),
        scratch_shapes=[pltpu.VMEM((2, TOP_K, tb, D), F32), pltpu.SemaphoreType.DMA((2,))],
        compiler_params=_cparams(("arbitrary",), 40),
        name="combine",
    )(dest3, dest3, y_sorted, w, shared, x1, g, mods)


def moe_ffn(h2, hp, x1, p, g, mods, n_ctx, t_lat, layer):
    N, D = x1.shape
    tm = MOE_TM
    idx, w, rank, cnt = router(h2, p['router_wt'], p['router_b'])
    counts = cnt[:, 0].astype(I32)
    padded = (counts + tm - 1) // tm * tm
    pend = jnp.cumsum(padded)
    pstart = pend - padded
    n_blocks = (N * TOP_K + N_EXPERTS * (tm - 1) + tm - 1) // tm
    n_used = (pend[-1] // tm).astype(I32).reshape(1)
    blk_start = jnp.arange(n_blocks, dtype=I32) * tm
    blk_e = jnp.minimum(jnp.sum((pend[None, :] <= blk_start[:, None]).astype(I32), axis=1), N_EXPERTS - 1)
    hit = idx[:, :, None] == jnp.arange(N_EXPERTS, dtype=I32)[None, None, :]
    dest = jnp.sum(jnp.where(hit, pstart[None, None, :], 0), axis=-1) + rank
    xs = dispatch(hp, dest, n_blocks * tm)
    hid = expert_up(xs, blk_e, n_used, p['exp_w_gate'], p['exp_w_up'], layer)
    ys = expert_down(hid, blk_e, n_used, p['exp_w_down'], layer)
    sh_hid = swiglu(h2, p['sh_w_gate'], p['sh_w_up'])
    shared = matmul(sh_hid, p['sh_w_down'])
    return combine(ys, dest, w.T, shared, x1, g, mods, n_ctx, t_lat)


def _axial_rope_tables(n):
    rows = n // GRID_W
    t = jnp.arange(rows * GRID_W)
    row = (t // GRID_W).astype(F32)
    col = (t % GRID_W).astype(F32)
    n_freq = LANES // 4
    inv = ROPE_BASE ** (-jnp.arange(n_freq, dtype=F32) / n_freq)
    ang = jnp.concatenate([row[:, None] * inv, col[:, None] * inv], axis=-1)
    cos, sin = jnp.cos(ang), jnp.sin(ang)
    return jnp.concatenate([cos, cos], axis=-1), jnp.concatenate([-sin, sin], axis=-1)


def _layer_params(l, w_in, conv_w, conv_b, lru_wa, lru_ba, lru_wx, lru_bx, lru_lam, diff_lam, diff_subln,
                  gla_w, gla_b, gla_norm, w_out, router_w, router_b, exp_w_gate, exp_w_up, exp_w_down,
                  sh_w_gate, sh_w_up, sh_w_down):
    d_lru = conv_w.shape[-1]
    n_main = w_in.shape[-1] - 2 * GLA_RANK
    dk = LANES
    wa, wx = lru_wa[l], lru_wx[l]
    w_gate = jnp.concatenate([wa[0], wx[0], wa[1], wx[1]], axis=-1).astype(BF16)
    ba = lru_ba[l].reshape(2, H_LRU, 1, -1)
    bx = lru_bx[l].reshape(2, H_LRU, 1, -1)
    b_gate = jnp.concatenate([ba[0], bx[0], ba[1], bx[1]], axis=-1)
    gw = gla_w[l]
    w_glog = jnp.zeros((H_GLA, LANES, 2 * dk), F32)
    for d in range(2):
        blk = gw[d].reshape(GLA_RANK, H_GLA, dk).transpose(1, 0, 2)
        w_glog = w_glog.at[:, d * GLA_RANK:(d + 1) * GLA_RANK, d * dk:(d + 1) * dk].set(blk)
    gb = gla_b[l].reshape(2, H_GLA, 1, dk)
    b_glog = jnp.concatenate([gb[0], gb[1]], axis=-1)
    w_lr = jnp.zeros((w_in.shape[1], LANES), F32).at[:, :2 * GLA_RANK].set(w_in[l][:, n_main:])
    return {
        'w_main': w_in[l][:, :n_main].astype(BF16), 'w_lr': w_lr.astype(BF16),
        'conv_w': conv_w[l], 'conv_b': conv_b[l].reshape(1, d_lru),
        'w_gate': w_gate, 'b_gate': b_gate, 'lru_lam': lru_lam[l],
        'diff_lam': diff_lam[l], 'diff_subln': diff_subln[l].reshape(1, -1),
        'w_glog': w_glog.astype(BF16), 'b_glog': b_glog, 'gla_norm': gla_norm[l].reshape(1, -1),
        'w_out': w_out[l].astype(BF16),
        'router_wt': router_w[l].T.astype(BF16),
        'router_b': jnp.broadcast_to(router_b[l][:, None], (N_EXPERTS, LANES)),
        'exp_w_gate': exp_w_gate, 'exp_w_up': exp_w_up, 'exp_w_down': exp_w_down,
        'sh_w_gate': sh_w_gate[l].astype(BF16), 'sh_w_up': sh_w_up[l].astype(BF16),
        'sh_w_down': sh_w_down[l].astype(BF16),
    }


def kernel(x_prompt, x_sample, cache_k, cache_v, state_lru, state_gla, c, c_ctx, mod_w, mod_b, norm_g, w_in,
           conv_w, conv_b, lru_wa, lru_ba, lru_wx, lru_bx, lru_lam, diff_lam, diff_subln, gla_w, gla_b,
           gla_norm, w_out, router_w, router_b, exp_w_gate, exp_w_up, exp_w_down, sh_w_gate, sh_w_up,
           sh_w_down):
    Bc, Tc, D = x_prompt.shape
    Bl, Tl, _ = x_sample.shape
    L = mod_w.shape[0]
    n_ctx, n_lat = Bc * Tc, Bl * Tl
    d_lru = conv_w.shape[-1]
    d_diff = H_DIFF * 2 * LANES
    d_gk = H_GLA * LANES
    d_gv = H_GLA * 2 * LANES
    col_qb = 2 * d_lru
    col_qc = col_qb + 3 * d_diff
    col_kc = col_qc + d_gk
    col_vc = col_kc + d_gk
    col_gc = col_vc + d_gv

    x = jnp.concatenate([x_prompt.reshape(n_ctx, D), x_sample.reshape(n_lat, D)], axis=0)
    n_rows = 16 * ((1 + Bl + 15) // 16)
    c_rows = jnp.zeros((n_rows, D), F32).at[0].set(c_ctx).at[1:1 + Bl].set(c)
    mods_all = modulation(c_rows, mod_w, mod_b).reshape(L, n_rows, 6, D)
    rope = _axial_rope_tables(Tl)
    cache_k2 = cache_k.reshape(*cache_k.shape[:3], -1)
    cache_v2 = cache_v.reshape(*cache_v.shape[:3], -1)
    zeros_lru = jnp.zeros((Bc, 2, d_lru), F32)
    zeros_gla = jnp.zeros((Bc, 2, H_GLA, LANES, 2 * LANES), F32)

    ks, vs, lrus, glas = [], [], [], []
    for l in range(L):
        p = _layer_params(l, w_in, conv_w, conv_b, lru_wa, lru_ba, lru_wx, lru_bx, lru_lam, diff_lam,
                          diff_subln, gla_w, gla_b, gla_norm, w_out, router_w, router_b, exp_w_gate,
                          exp_w_up, exp_w_down, sh_w_gate, sh_w_up, sh_w_down)
        lam_init = 0.8 - 0.6 * math.exp(-0.3 * l)
        mods = mods_all[l]
        g = norm_g[l]
        h = norm_mod(x, g[0:1], mods, n_ctx, Tl)
        z = matmul(h, p['w_main'])
        z_lr = matmul(h, p['w_lr'])
        ycat = jnp.zeros((n_ctx + n_lat, D), BF16)
        ycat, lru_c = lru_mixer(z, ycat, 0, Bc, Tc, p['conv_w'], p['conv_b'], p['w_gate'], p['b_gate'],
                                p['lru_lam'], zeros_lru)
        ycat, _ = lru_mixer(z, ycat, n_ctx, Bl, Tl, p['conv_w'], p['conv_b'], p['w_gate'], p['b_gate'],
                            p['lru_lam'], state_lru[:, l])
        ycat = diff_attention(z, ycat, d_lru, 0, Bc, Tc, col_qb, None, None, l, None, p['diff_lam'],
                              p['diff_subln'], lam_init)
        ycat = diff_attention(z, ycat, d_lru, n_ctx, Bl, Tl, col_qb, cache_k2, cache_v2, l, rope, p['diff_lam'],
                              p['diff_subln'], lam_init)
        ycat, gla_c = gla_mixer(z, z_lr, ycat, d_lru + d_diff, 0, Bc, Tc, col_qc, col_kc, col_vc, col_gc,
                                p['w_glog'], p['b_glog'], p['gla_norm'], zeros_gla)
        ycat, _ = gla_mixer(z, z_lr, ycat, d_lru + d_diff, n_ctx, Bl, Tl, col_qc, col_kc, col_vc, col_gc,
                            p['w_glog'], p['b_glog'], p['gla_norm'], state_gla[:, l])
        y = matmul(ycat, p['w_out'])
        x1, h2, hp = post_mix(x, y, g, mods, n_ctx, Tl)
        x = moe_ffn(h2, hp, x1, p, g, mods, n_ctx, Tl, l)
        ks.append(z[:n_ctx, col_qb + d_diff:col_qb + 2 * d_diff].reshape(Bc, Tc, H_DIFF, 2 * LANES))
        vs.append(z[:n_ctx, col_qb + 2 * d_diff:col_qb + 3 * d_diff].reshape(Bc, Tc, H_DIFF, 2 * LANES))
        lrus.append(lru_c)
        glas.append(gla_c)
    yp = x[:n_ctx].reshape(Bc, Tc, D)
    ys = x[n_ctx:].reshape(Bl, Tl, D)
    return (yp, ys, jnp.stack(ks, axis=1), jnp.stack(vs, axis=1), jnp.stack(lrus, axis=1), jnp.stack(glas, axis=1))
```

```python
import functools
import math

import jax
import jax.numpy as jnp
from jax import lax
from jax.experimental import pallas as pl
from jax.experimental.pallas import tpu as pltpu

F32 = jnp.float32
BF16 = jnp.bfloat16
U32 = jnp.uint32
I32 = jnp.int32

EPS = 1e-6
GRID_W = 64
H_LRU = 8
CONV_W = 4
CONV_LEFT = 2
LRU_C = 8.0
H_DIFF = 8
ROPE_BASE = 10000.0
H_GLA = 4
GLA_RANK = 16
GLA_TAU = 16.0
GLA_CHUNK = 64
N_EXPERTS = 64
TOP_K = 8
N_GROUPS = 8
TOPK_GROUPS = 4
ROUTED_SCALE = 2.5

V7X_VMEM_BYTES = 64 * 1024 * 1024
LANES = 128
SUBLANES = 8
MIB = 1024 * 1024


def _cparams(semantics, vmem_mib):
    assert vmem_mib * MIB < V7X_VMEM_BYTES
    return pltpu.CompilerParams(dimension_semantics=semantics, vmem_limit_bytes=vmem_mib * MIB)


def _rms(x):
    return x * lax.rsqrt(jnp.mean(x * x, axis=-1, keepdims=True) + EPS)


def _sigmoid(x):
    return 1.0 / (1.0 + jnp.exp(-x))


def _silu(x):
    return x * _sigmoid(x)


def _dot(a, b):
    return jnp.dot(a, b, preferred_element_type=F32)


def _dot_nt(a, b):
    return lax.dot_general(a, b, (((1,), (1,)), ((), ())), preferred_element_type=F32)


def _dot_tn(a, b):
    return lax.dot_general(a, b, (((0,), (0,)), ((), ())), preferred_element_type=F32)


def _mod_kernel(c_ref, w_ref, b_ref, o_ref):
    s = _silu(c_ref[...]).astype(BF16)
    o_ref[...] = _dot(s, w_ref[...].astype(BF16)) + b_ref[...]


def modulation(c_rows, mod_w, mod_b):
    L, D, N6 = mod_w.shape
    R = c_rows.shape[0]
    tn = 512
    return pl.pallas_call(
        _mod_kernel,
        grid=(L, N6 // tn),
        in_specs=[
            pl.BlockSpec((R, D), lambda l, j: (0, 0)),
            pl.BlockSpec((None, D, tn), lambda l, j: (l, 0, j)),
            pl.BlockSpec((None, 1, tn), lambda l, j: (l, 0, j)),
        ],
        out_specs=pl.BlockSpec((None, R, tn), lambda l, j: (l, 0, j)),
        out_shape=jax.ShapeDtypeStruct((L, R, N6), F32),
        compiler_params=_cparams(("arbitrary", "arbitrary"), 40),
        name="modulation",
    )(c_rows, mod_w, mod_b.reshape(L, 1, N6))


def _mod_row_index(i, tm, n_ctx, t_lat):
    r0 = i * tm
    return jnp.where(r0 < n_ctx, 0, 1 + (r0 - n_ctx) // t_lat)


def _norm_mod_kernel(x_ref, g_ref, mod_ref, o_ref):
    y = _rms(x_ref[...]) * g_ref[...]
    o_ref[...] = (y * (1.0 + mod_ref[1:2, :]) + mod_ref[0:1, :]).astype(o_ref.dtype)


def norm_mod(x, g, mods, n_ctx, t_lat):
    N, D = x.shape
    tm = 256
    return pl.pallas_call(
        _norm_mod_kernel,
        grid=(N // tm,),
        in_specs=[
            pl.BlockSpec((tm, D), lambda i: (i, 0)),
            pl.BlockSpec((1, D), lambda i: (0, 0)),
            pl.BlockSpec((None, 6, D), lambda i: (_mod_row_index(i, tm, n_ctx, t_lat), 0, 0)),
        ],
        out_specs=pl.BlockSpec((tm, D), lambda i: (i, 0)),
        out_shape=jax.ShapeDtypeStruct((N, D), BF16),
        compiler_params=_cparams(("arbitrary",), 32),
        name="norm_mod",
    )(x, g, mods)


def _mm_kernel(x_ref, w_ref, o_ref):
    o_ref[...] = _dot(x_ref[...], w_ref[...]).astype(o_ref.dtype)


def matmul(x, w, out_dtype=F32, tm=1024, tn=512):
    M, K = x.shape
    N = w.shape[1]
    tm = min(tm, M)
    tn = min(tn, N)
    assert M % tm == 0 and N % tn == 0
    return pl.pallas_call(
        _mm_kernel,
        grid=(M // tm, N // tn),
        in_specs=[pl.BlockSpec((tm, K), lambda i, j: (i, 0)), pl.BlockSpec((K, tn), lambda i, j: (0, j))],
        out_specs=pl.BlockSpec((tm, tn), lambda i, j: (i, j)),
        out_shape=jax.ShapeDtypeStruct((M, N), out_dtype),
        compiler_params=_cparams(("arbitrary", "arbitrary"), 48),
        name="matmul",
    )(x, w)


def _swiglu_kernel(x_ref, wg_ref, wu_ref, o_ref):
    x = x_ref[...]
    o_ref[...] = (_silu(_dot(x, wg_ref[...])) * _dot(x, wu_ref[...])).astype(o_ref.dtype)


def swiglu(x, wg, wu, tm=1024, tn=256):
    M, K = x.shape
    N = wg.shape[1]
    tm = min(tm, M)
    return pl.pallas_call(
        _swiglu_kernel,
        grid=(M // tm, N // tn),
        in_specs=[
            pl.BlockSpec((tm, K), lambda i, j: (i, 0)),
            pl.BlockSpec((K, tn), lambda i, j: (0, j)),
            pl.BlockSpec((K, tn), lambda i, j: (0, j)),
        ],
        out_specs=pl.BlockSpec((tm, tn), lambda i, j: (i, j)),
        out_shape=jax.ShapeDtypeStruct((M, N), BF16),
        compiler_params=_cparams(("arbitrary", "arbitrary"), 48),
        name="swiglu",
    )(x, wg, wu)


def _lru_kernel(xa_ref, ga_ref, cw_ref, cb_ref, wg_ref, bg_ref, lam_ref, h0_ref, ybuf_ref, y_ref, hl_ref,
                pad_ref, af_ref, uf_ref, ab_ref, ub_ref, *, T, CW):
    del ybuf_ref
    pad_ref[0:SUBLANES, :] = jnp.zeros((SUBLANES, CW), F32)
    pad_ref[SUBLANES + T:2 * SUBLANES + T, :] = jnp.zeros((SUBLANES, CW), F32)
    pad_ref[SUBLANES:SUBLANES + T, :] = xa_ref[...]
    xc = cb_ref[...] + jnp.zeros((T, CW), F32)
    for j in range(CONV_W):
        o = SUBLANES - CONV_LEFT + j
        xc = xc + cw_ref[j:j + 1, :] * pad_ref[o:o + T, :]
    lam = lam_ref[...]
    nl = -lam
    softplus = jnp.maximum(nl, 0.0) + jnp.log(1.0 + jnp.exp(-jnp.abs(nl)))
    bs = LANES
    for hh in range(CW // bs):
        cs = slice(hh * bs, (hh + 1) * bs)
        xb = xc[:, cs]
        gts = _dot(xb.astype(BF16), wg_ref[hh]) + bg_ref[hh]
        for d, (a_ref, u_ref) in enumerate(((af_ref, uf_ref), (ab_ref, ub_ref))):
            r = _sigmoid(gts[:, (2 * d) * bs:(2 * d + 1) * bs])
            ig = _sigmoid(gts[:, (2 * d + 1) * bs:(2 * d + 2) * bs])
            log_a = (-LRU_C) * r * softplus[d:d + 1, cs]
            a = jnp.exp(log_a)
            a_ref[:, cs] = a
            u_ref[:, cs] = jnp.sqrt(1.0 - a * a) * (ig * xb)

    def step(t, carry):
        hf, hb = carry
        hf = af_ref[pl.ds(t, 1), :] * hf + uf_ref[pl.ds(t, 1), :]
        uf_ref[pl.ds(t, 1), :] = hf
        tb = T - 1 - t
        hb = ab_ref[pl.ds(tb, 1), :] * hb + ub_ref[pl.ds(tb, 1), :]
        ub_ref[pl.ds(tb, 1), :] = hb
        return hf, hb

    hf, hb = lax.fori_loop(0, T, step, (h0_ref[0:1, :], h0_ref[1:2, :]), unroll=8)
    hl_ref[0:1, :] = hf
    hl_ref[1:2, :] = hb
    y_ref[...] = ((uf_ref[...] + ub_ref[...]) * jax.nn.gelu(ga_ref[...], approximate=True)).astype(y_ref.dtype)


def lru_mixer(z, ybuf, row0, B, T, conv_w, conv_b, w_gate, b_gate, lam, h0):
    d_lru = conv_w.shape[1]
    CW = 512
    nj = d_lru // CW
    rb0 = row0 // T
    kern = functools.partial(_lru_kernel, T=T, CW=CW)
    return pl.pallas_call(
        kern,
        grid=(B, nj),
        in_specs=[
            pl.BlockSpec((T, CW), lambda b, j: (rb0 + b, j)),
            pl.BlockSpec((T, CW), lambda b, j: (rb0 + b, nj + j)),
            pl.BlockSpec((CONV_W, CW), lambda b, j: (0, j)),
            pl.BlockSpec((1, CW), lambda b, j: (0, j)),
            pl.BlockSpec((CW // LANES, LANES, 4 * LANES), lambda b, j: (j, 0, 0)),
            pl.BlockSpec((CW // LANES, 1, 4 * LANES), lambda b, j: (j, 0, 0)),
            pl.BlockSpec((2, CW), lambda b, j: (0, j)),
            pl.BlockSpec((None, 2, CW), lambda b, j: (b, 0, j)),
            pl.BlockSpec(memory_space=pl.ANY),
        ],
        out_specs=[
            pl.BlockSpec((T, CW), lambda b, j: (rb0 + b, j)),
            pl.BlockSpec((None, 2, CW), lambda b, j: (b, 0, j)),
        ],
        out_shape=[jax.ShapeDtypeStruct(ybuf.shape, ybuf.dtype), jax.ShapeDtypeStruct((B, 2, d_lru), F32)],
        scratch_shapes=[pltpu.VMEM((T + 2 * SUBLANES, CW), F32)] + [pltpu.VMEM((T, CW), F32)] * 4,
        input_output_aliases={8: 0},
        compiler_params=_cparams(("arbitrary", "arbitrary"), 40),
        name="lru_mixer",
    )(z, z, conv_w, conv_b, w_gate, b_gate, lam, h0, ybuf)


def _rope(x, cos2, sin2):
    return x * cos2 + pltpu.roll(x, LANES // 2, axis=1) * sin2


def _attn_kernel(*refs, T, P, tq, use_rope, lam_init):
    it = iter(refs)
    q_ref, k_ref, v_ref = next(it), next(it), next(it)
    ck_ref = cv_ref = cos_ref = sin_ref = None
    if P:
        ck_ref, cv_ref = next(it), next(it)
    if use_rope:
        cos_ref, sin_ref = next(it), next(it)
    dl_ref, sub_ref, _, o_ref, kall_ref, vall_ref = next(it), next(it), next(it), next(it), next(it), next(it)
    dh = LANES
    qi = pl.program_id(2)

    @pl.when(qi == 0)
    def _():
        if P:
            kall_ref[0:P, :] = ck_ref[...].astype(BF16)
            vall_ref[0:P, :] = cv_ref[...].astype(BF16)
        for m in range(2):
            km = k_ref[:, m * dh:(m + 1) * dh]
            if use_rope:
                km = _rope(km, cos_ref[...], sin_ref[...])
            kall_ref[P:P + T, m * dh:(m + 1) * dh] = km.astype(BF16)
        vall_ref[P:P + T, :] = v_ref[...].astype(BF16)

    dl = dl_ref[...]
    lam = (jnp.exp(jnp.sum(dl[0:1, :] * dl[1:2, :], axis=-1, keepdims=True))
           - jnp.exp(jnp.sum(dl[2:3, :] * dl[3:4, :], axis=-1, keepdims=True)) + lam_init)
    qscale = (dh ** -0.5) * math.log2(math.e)
    pv, invs = [], []
    for m in range(2):
        qm = q_ref[:, m * dh:(m + 1) * dh]
        if use_rope:
            r0 = pl.multiple_of(qi * tq, tq)
            qm = _rope(qm, cos_ref[pl.ds(r0, tq), :], sin_ref[pl.ds(r0, tq), :])
        s2 = _dot_nt((qm * qscale).astype(BF16), kall_ref[:, m * dh:(m + 1) * dh])
        e = jnp.exp2(s2 - jnp.max(s2, axis=-1, keepdims=True))
        invs.append(1.0 / jnp.sum(e, axis=-1, keepdims=True))
        pv.append(_dot(e.astype(BF16), vall_ref[...]))
    o = pv[0] * invs[0] - pv[1] * (lam * invs[1])
    o_ref[...] = (_rms(o) * sub_ref[...] * (1.0 - lam_init)).astype(o_ref.dtype)


def diff_attention(z, ybuf, col_y, row0, B, T, col_q, cache_k, cache_v, layer, rope, diff_lam, subln, lam_init):
    H = H_DIFF
    dw = 2 * LANES
    P = 0 if cache_k is None else cache_k.shape[2]
    tq = min(T, 256)
    nq = T // tq
    rbq = row0 // tq
    rbk = row0 // T
    cq = col_q // dw
    kern = functools.partial(_attn_kernel, T=T, P=P, tq=tq, use_rope=rope is not None, lam_init=lam_init)
    in_specs = [
        pl.BlockSpec((tq, dw), lambda b, h, i: (rbq + b * nq + i, cq + h)),
        pl.BlockSpec((T, dw), lambda b, h, i: (rbk + b, cq + H + h)),
        pl.BlockSpec((T, dw), lambda b, h, i: (rbk + b, cq + 2 * H + h)),
    ]
    args = [z, z, z]
    if P:
        in_specs += [pl.BlockSpec((None, None, P, dw), lambda b, h, i: (b, layer, 0, h))] * 2
        args += [cache_k, cache_v]
    if rope is not None:
        in_specs += [pl.BlockSpec((T, LANES), lambda b, h, i: (0, 0))] * 2
        args += [rope[0], rope[1]]
    in_specs += [pl.BlockSpec((4, LANES), lambda b, h, i: (0, 0)), pl.BlockSpec((1, dw), lambda b, h, i: (0, 0)),
                 pl.BlockSpec(memory_space=pl.ANY)]
    args += [diff_lam, subln, ybuf]
    cy = col_y // dw
    return pl.pallas_call(
        kern,
        grid=(B, H, nq),
        in_specs=in_specs,
        out_specs=pl.BlockSpec((tq, dw), lambda b, h, i: (rbq + b * nq + i, cy + h)),
        out_shape=jax.ShapeDtypeStruct(ybuf.shape, ybuf.dtype),
        scratch_shapes=[pltpu.VMEM((P + T, dw), BF16), pltpu.VMEM((P + T, dw), BF16)],
        input_output_aliases={len(args) - 1: 0},
        compiler_params=_cparams(("arbitrary", "arbitrary", "arbitrary"), 40),
        name="diff_attention",
    )(*args)


def _gla_kernel(q_ref, k_ref, v_ref, gc_ref, lr_ref, wgl_ref, bgl_ref, gn_ref, s0_ref, ybuf_ref, y_ref, so_ref,
                glog_ref, qt_ref, ec_ref, kv_ref, o_ref, s_ref, *, T):
    del ybuf_ref
    C = GLA_CHUNK
    dk = LANES
    nc = T // C
    gl = _dot(lr_ref[...].astype(BF16), wgl_ref[...]) + bgl_ref[...]
    glog_ref[...] = (jnp.minimum(gl, 0.0) - jnp.log(1.0 + jnp.exp(-jnp.abs(gl)))) * (1.0 / GLA_TAU)
    row = lax.broadcasted_iota(I32, (C, C), 0)
    col = lax.broadcasted_iota(I32, (C, C), 1)
    qscale = dk ** -0.5
    ones = jnp.ones((C, dk), BF16)

    def prepare(c, carry):
        r0 = pl.multiple_of(c * C, C)
        q = q_ref[pl.ds(r0, C), :] * qscale
        k = k_ref[pl.ds(r0, C), :]
        v = v_ref[pl.ds(r0, C), :].astype(BF16)
        for d in range(2):
            keep = (row >= col) if d == 0 else (row <= col)
            tri = jnp.where(keep, 1.0, 0.0).astype(BF16)
            g = glog_ref[pl.ds(r0, C), d * dk:(d + 1) * dk]
            g_hi = g.astype(BF16)
            g_lo = (g - g_hi.astype(F32)).astype(BF16)
            b = _dot(tri, g_hi) + _dot(tri, g_lo)
            b_end = b[C - 1:C, :] if d == 0 else b[0:1, :]
            qt = (q * jnp.exp(b)).astype(BF16)
            kt = (k * jnp.exp(-b)).astype(BF16)
            att = jnp.where(keep, _dot_nt(qt, kt), 0.0)
            qt_ref[d, pl.ds(r0, C), :] = qt
            o_ref[d, pl.ds(r0, C), :] = _dot(att.astype(BF16), v)
            kd = (k * jnp.exp(b_end - b)).astype(BF16)
            bcol = _dot_tn(g_hi, ones) + _dot_tn(g_lo, ones)
            ec_ref[d, c] = jnp.exp(bcol)
            kv_ref[d, c] = _dot_tn(kd, v)
        return carry

    lax.fori_loop(0, nc, prepare, 0, unroll=2)
    s_ref[...] = s0_ref[...]

    def advance(ci, carry):
        for d in range(2):
            c = ci if d == 0 else nc - 1 - ci
            r0 = pl.multiple_of(c * C, C)
            S = s_ref[d]
            o_ref[d, pl.ds(r0, C), :] = o_ref[d, pl.ds(r0, C), :] + _dot(qt_ref[d, pl.ds(r0, C), :], S.astype(BF16))
            ec = ec_ref[d, c]
            s_ref[d] = jnp.concatenate([ec, ec], axis=1) * S + kv_ref[d, c]
        return carry

    lax.fori_loop(0, nc, advance, 0)
    so_ref[...] = s_ref[...]
    oc = o_ref[0] + o_ref[1]
    y_ref[...] = (_rms(oc) * gn_ref[...] * _silu(gc_ref[...])).astype(y_ref.dtype)


def gla_mixer(z, z_lr, ybuf, col_y, row0, B, T, col_q, col_k, col_v, col_g, w_glog, b_glog, gnorm, s0):
    H = H_GLA
    dk, dv = LANES, 2 * LANES
    rb = row0 // T
    kern = functools.partial(_gla_kernel, T=T)
    return pl.pallas_call(
        kern,
        grid=(B, H),
        in_specs=[
            pl.BlockSpec((T, dk), lambda b, h: (rb + b, col_q // dk + h)),
            pl.BlockSpec((T, dk), lambda b, h: (rb + b, col_k // dk + h)),
            pl.BlockSpec((T, dv), lambda b, h: (rb + b, col_v // dv + h)),
            pl.BlockSpec((T, dv), lambda b, h: (rb + b, col_g // dv + h)),
            pl.BlockSpec((T, LANES), lambda b, h: (rb + b, 0)),
            pl.BlockSpec((None, LANES, 2 * dk), lambda b, h: (h, 0, 0)),
            pl.BlockSpec((None, 1, 2 * dk), lambda b, h: (h, 0, 0)),
            pl.BlockSpec((1, dv), lambda b, h: (0, 0)),
            pl.BlockSpec((None, 2, None, dk, dv), lambda b, h: (b, 0, h, 0, 0)),
            pl.BlockSpec(memory_space=pl.ANY),
        ],
        out_specs=[
            pl.BlockSpec((T, dv), lambda b, h: (rb + b, col_y // dv + h)),
            pl.BlockSpec((None, 2, None, dk, dv), lambda b, h: (b, 0, h, 0, 0)),
        ],
        out_shape=[jax.ShapeDtypeStruct(ybuf.shape, ybuf.dtype), jax.ShapeDtypeStruct((B, 2, H, dk, dv), F32)],
        scratch_shapes=[pltpu.VMEM((T, 2 * dk), F32), pltpu.VMEM((2, T, dk), BF16),
                        pltpu.VMEM((2, T // GLA_CHUNK, dk, dk), F32), pltpu.VMEM((2, T // GLA_CHUNK, dk, dv), F32),
                        pltpu.VMEM((2, T, dv), F32), pltpu.VMEM((2, dk, dv), F32)],
        input_output_aliases={9: 0},
        compiler_params=_cparams(("arbitrary", "arbitrary"), 32),
        name="gla_mixer",
    )(z, z, z, z, z_lr, w_glog, b_glog, gnorm, s0, ybuf)


def _post_mix_kernel(x_ref, y_ref, g_ref, mod_ref, x1_ref, h2_ref, hp_ref):
    D = x_ref.shape[1]
    x1 = x_ref[...] + mod_ref[2:3, :] * (_rms(y_ref[...]) * g_ref[1:2, :])
    x1_ref[...] = x1
    h = (_rms(x1) * g_ref[2:3, :]) * (1.0 + mod_ref[4:5, :]) + mod_ref[3:4, :]
    hb = h.astype(BF16)
    h2_ref[...] = hb
    u = pltpu.bitcast(hb.astype(F32), U32)
    hp_ref[...] = (u[:, :D // 2] >> 16) | (u[:, D // 2:] & jnp.uint32(0xFFFF0000))


def post_mix(x, y, g, mods, n_ctx, t_lat):
    N, D = x.shape
    tm = 256
    return pl.pallas_call(
        _post_mix_kernel,
        grid=(N // tm,),
        in_specs=[
            pl.BlockSpec((tm, D), lambda i: (i, 0)),
            pl.BlockSpec((tm, D), lambda i: (i, 0)),
            pl.BlockSpec((4, D), lambda i: (0, 0)),
            pl.BlockSpec((None, 6, D), lambda i: (_mod_row_index(i, tm, n_ctx, t_lat), 0, 0)),
        ],
        out_specs=[
            pl.BlockSpec((tm, D), lambda i: (i, 0)),
            pl.BlockSpec((tm, D), lambda i: (i, 0)),
            pl.BlockSpec((tm, D // 2), lambda i: (i, 0)),
        ],
        out_shape=[jax.ShapeDtypeStruct((N, D), F32), jax.ShapeDtypeStruct((N, D), BF16),
                   jax.ShapeDtypeStruct((N, D // 2), U32)],
        compiler_params=_cparams(("arbitrary",), 48),
        name="post_mix",
    )(x, y, g, mods)


def _router_kernel(h_ref, rwt_ref, rb_ref, upper_ref, idx_ref, w_ref, rank_ref, cnt_ref, carry_ref, *, tm):
    i = pl.program_id(0)
    per = N_EXPERTS // N_GROUPS

    @pl.when(i == 0)
    def _():
        carry_ref[...] = jnp.zeros_like(carry_ref)

    logits = _dot_nt(rwt_ref[...], h_ref[...])
    s_all = _sigmoid(logits)
    sel_all = s_all + rb_ref[...][:, 0:1]
    sub = lax.broadcasted_iota(I32, (per, tm), 0)
    neg = -jnp.inf
    s_g, sel_g, gscore = [], [], []
    for g in range(N_GROUPS):
        blk = sel_all[g * per:(g + 1) * per, :]
        s_g.append(s_all[g * per:(g + 1) * per, :])
        sel_g.append(blk)
        m1 = jnp.max(blk, axis=0, keepdims=True)
        first = jnp.min(jnp.where(blk == m1, sub, per), axis=0, keepdims=True)
        m2 = jnp.max(jnp.where(sub == first, neg, blk), axis=0, keepdims=True)
        gscore.append(m1 + m2)
    for g in range(N_GROUPS):
        beat = jnp.zeros((1, tm), F32)
        for g2 in range(N_GROUPS):
            if g2 == g:
                continue
            wins = (gscore[g2] > gscore[g]) | ((gscore[g2] == gscore[g]) & (g2 < g))
            beat = beat + jnp.where(wins, 1.0, 0.0)
        sel_g[g] = jnp.where(beat < TOPK_GROUPS, sel_g[g], neg)
    eidx = [sub + g * per for g in range(N_GROUPS)]
    onehot = [jnp.zeros((per, tm), F32) for _ in range(N_GROUPS)]
    picks, wts = [], []
    for k in range(TOP_K):
        m = sel_g[0]
        for g in range(1, N_GROUPS):
            m = jnp.maximum(m, sel_g[g])
        m = jnp.max(m, axis=0, keepdims=True)
        cand = jnp.where(sel_g[0] == m, eidx[0], N_EXPERTS)
        for g in range(1, N_GROUPS):
            cand = jnp.minimum(cand, jnp.where(sel_g[g] == m, eidx[g], N_EXPERTS))
        pick = jnp.min(cand, axis=0, keepdims=True)
        wk = jnp.zeros((per, tm), F32)
        for g in range(N_GROUPS):
            hit = eidx[g] == pick
            wk = wk + jnp.where(hit, s_g[g], 0.0)
            onehot[g] = onehot[g] + jnp.where(hit, 1.0, 0.0)
            sel_g[g] = jnp.where(hit, neg, sel_g[g])
        picks.append(pick)
        wts.append(jnp.sum(wk, axis=0, keepdims=True))
    wsum = wts[0]
    for k in range(1, TOP_K):
        wsum = wsum + wts[k]
    oh = jnp.concatenate(onehot, axis=0)
    carry = carry_ref[...][:, 0:1]
    before = _dot(oh.astype(BF16), upper_ref[...]) + carry
    for k in range(TOP_K):
        idx_ref[k:k + 1, :] = picks[k]
        w_ref[k:k + 1, :] = wts[k] / wsum * ROUTED_SCALE
        rk = jnp.zeros((per, tm), F32)
        for g in range(N_GROUPS):
            rk = rk + jnp.where(eidx[g] == picks[k], before[g * per:(g + 1) * per, :], 0.0)
        rank_ref[k:k + 1, :] = jnp.sum(rk, axis=0, keepdims=True).astype(I32)
    carry_ref[...] = carry_ref[...] + jnp.sum(oh, axis=1, keepdims=True)
    cnt_ref[...] = carry_ref[...]


def router(h2, router_wt, router_b):
    N, D = h2.shape
    tm = 512 if N % 512 == 0 else 256
    E = N_EXPERTS
    upper = (jnp.arange(tm)[:, None] < jnp.arange(tm)[None, :]).astype(BF16)
    kern = functools.partial(_router_kernel, tm=tm)
    return pl.pallas_call(
        kern,
        grid=(N // tm,),
        in_specs=[
            pl.BlockSpec((tm, D), lambda i: (i, 0)),
            pl.BlockSpec((E, D), lambda i: (0, 0)),
            pl.BlockSpec((E, LANES), lambda i: (0, 0)),
            pl.BlockSpec((tm, tm), lambda i: (0, 0)),
        ],
        out_specs=[
            pl.BlockSpec((TOP_K, tm), lambda i: (0, i)),
            pl.BlockSpec((TOP_K, tm), lambda i: (0, i)),
            pl.BlockSpec((TOP_K, tm), lambda i: (0, i)),
            pl.BlockSpec((E, LANES), lambda i: (0, 0)),
        ],
        out_shape=[jax.ShapeDtypeStruct((TOP_K, N), I32), jax.ShapeDtypeStruct((TOP_K, N), F32),
                   jax.ShapeDtypeStruct((TOP_K, N), I32), jax.ShapeDtypeStruct((E, LANES), F32)],
        scratch_shapes=[pltpu.VMEM((E, LANES), F32)],
        compiler_params=_cparams(("arbitrary",), 32),
        name="router",
    )(h2, router_wt, router_b, upper)


MOE_TM = 256


def _dispatch_kernel(fill_ref, dest_ref, src_ref, xs_ref, zero_ref, sem, zsem, *, tb, n_fill):
    @pl.when(pl.program_id(0) == 0)
    def _():
        zero_ref[...] = jnp.zeros_like(zero_ref)

        def zissue(r, c):
            pltpu.make_async_copy(zero_ref.at[pl.ds(0, 1)], xs_ref.at[pl.ds(fill_ref[r], 1)], zsem).start()
            return c

        lax.fori_loop(0, n_fill, zissue, 0, unroll=8)

        def zdrain(r, c):
            pltpu.make_async_copy(zero_ref.at[pl.ds(0, 1)], xs_ref.at[pl.ds(0, 1)], zsem).wait()
            return c

        lax.fori_loop(0, n_fill, zdrain, 0, unroll=8)

    def issue(r, c):
        for k in range(TOP_K):
            d = dest_ref[0, k * tb + r]
            pltpu.make_async_copy(src_ref.at[pl.ds(r, 1)], xs_ref.at[pl.ds(d, 1)], sem).start()
        return c

    lax.fori_loop(0, tb, issue, 0, unroll=2)

    def drain(r, c):
        for k in range(TOP_K):
            pltpu.make_async_copy(src_ref.at[pl.ds(0, 1)], xs_ref.at[pl.ds(0, 1)], sem).wait()
        return c

    lax.fori_loop(0, tb, drain, 0, unroll=2)


def _block_major(dest, tb):
    K, N = dest.shape
    return dest.reshape(K, N // tb, tb).transpose(1, 0, 2).reshape(N // tb, 1, K * tb)


def dispatch(src, dest, fill):
    N, W = src.shape
    n_fill = fill.shape[0]
    n_slots = TOP_K * N + n_fill
    tb = 256
    kern = functools.partial(_dispatch_kernel, tb=tb, n_fill=n_fill)
    return pl.pallas_call(
        kern,
        grid_spec=pltpu.PrefetchScalarGridSpec(
            num_scalar_prefetch=1,
            grid=(N // tb,),
            in_specs=[
                pl.BlockSpec((None, 1, TOP_K * tb), lambda i, fl: (i, 0, 0), memory_space=pltpu.SMEM),
                pl.BlockSpec((tb, W), lambda i, fl: (i, 0)),
            ],
            out_specs=pl.BlockSpec(memory_space=pl.ANY),
            scratch_shapes=[pltpu.VMEM((SUBLANES, W), src.dtype), pltpu.SemaphoreType.DMA(()),
                            pltpu.SemaphoreType.DMA(())],
        ),
        out_shape=jax.ShapeDtypeStruct((n_slots, W), src.dtype),
        compiler_params=_cparams(("arbitrary",), 16),
        name="dispatch",
    )(fill, _block_major(dest, tb), src)


def _expert_up_kernel(be_ref, nu_ref, nxt_ref, xs_ref, wg_ref, wu_ref, h_ref, stage_ref, wgb_ref, wub_ref, slot_ref,
                      sem, *, layer, tn):
    j = pl.program_id(0)
    i = pl.program_id(1)
    nu = nu_ref[0]
    half = xs_ref.shape[1]
    e = be_ref[i]
    fresh = (i == 0) | (e != be_ref[jnp.maximum(i - 1, 0)])
    c0 = pl.multiple_of(j * tn, tn)

    def copies(expert, slot):
        return [pltpu.make_async_copy(w.at[layer, expert, :, pl.ds(c0, tn)], stage_ref.at[slot, m], sem.at[slot, m])
                for m, w in enumerate((wg_ref, wu_ref))]

    @pl.when(i == 0)
    def _():
        slot_ref[0] = 0
        for cp in copies(e, 0):
            cp.start()

    @pl.when((i < nu) & fresh)
    def _():
        slot = slot_ref[0]
        for cp in copies(e, slot):
            cp.wait()
        rows = stage_ref.shape[2] // 8
        for r in range(0, stage_ref.shape[2], rows):
            wgb_ref[r:r + rows, :] = stage_ref[slot, 0, r:r + rows, :].astype(BF16)
            wub_ref[r:r + rows, :] = stage_ref[slot, 1, r:r + rows, :].astype(BF16)
        nxt = nxt_ref[i]

        @pl.when(nxt >= 0)
        def _():
            for cp in copies(nxt, 1 - slot):
                cp.start()

        slot_ref[0] = 1 - slot

    @pl.when(i < nu)
    def _():
        u = xs_ref[...]
        lo = pltpu.bitcast(u << 16, F32).astype(BF16)
        hi = pltpu.bitcast(u & jnp.uint32(0xFFFF0000), F32).astype(BF16)
        g = _dot(lo, wgb_ref[0:half, :]) + _dot(hi, wgb_ref[half:2 * half, :])
        up = _dot(lo, wub_ref[0:half, :]) + _dot(hi, wub_ref[half:2 * half, :])
        h_ref[...] = (_silu(g) * up).astype(h_ref.dtype)

    @pl.when(i >= nu)
    def _():
        h_ref[...] = jnp.zeros_like(h_ref)


def expert_up(xs, blk_e, n_used, nxt_e, w_gate, w_up, layer):
    S, half = xs.shape
    _, E, D, Fd = w_gate.shape
    tm, tn = MOE_TM, 512
    nb = S // tm
    kern = functools.partial(_expert_up_kernel, layer=layer, tn=tn)
    return pl.pallas_call(
        kern,
        grid_spec=pltpu.PrefetchScalarGridSpec(
            num_scalar_prefetch=3,
            grid=(Fd // tn, nb),
            in_specs=[
                pl.BlockSpec((tm, half), lambda j, i, be, nu, nx: (jnp.minimum(i, nu[0] - 1), 0)),
                pl.BlockSpec(memory_space=pl.ANY),
                pl.BlockSpec(memory_space=pl.ANY),
            ],
            out_specs=pl.BlockSpec((tm, tn), lambda j, i, be, nu, nx: (i, j)),
            scratch_shapes=[pltpu.VMEM((2, 2, D, tn), F32), pltpu.VMEM((D, tn), BF16), pltpu.VMEM((D, tn), BF16),
                            pltpu.SMEM((1,), I32), pltpu.SemaphoreType.DMA((2, 2))],
        ),
        out_shape=jax.ShapeDtypeStruct((S, Fd), BF16),
        compiler_params=_cparams(("arbitrary", "arbitrary"), 62),
        name="expert_up",
    )(blk_e, n_used, nxt_e, xs, w_gate, w_up)


def _expert_down_kernel(be_ref, nu_ref, nxt_ref, h_ref, wd_ref, y_ref, stage_ref, wdb_ref, slot_ref, sem, *, layer):
    i = pl.program_id(0)
    nu = nu_ref[0]
    e = be_ref[i]
    fresh = (i == 0) | (e != be_ref[jnp.maximum(i - 1, 0)])

    def copy(expert, slot):
        return pltpu.make_async_copy(wd_ref.at[layer, expert], stage_ref.at[slot], sem.at[slot])

    @pl.when(i == 0)
    def _():
        slot_ref[0] = 0
        copy(e, 0).start()

    @pl.when((i < nu) & fresh)
    def _():
        slot = slot_ref[0]
        copy(e, slot).wait()
        rows = stage_ref.shape[1] // 8
        for r in range(0, stage_ref.shape[1], rows):
            wdb_ref[r:r + rows, :] = stage_ref[slot, r:r + rows, :].astype(BF16)
        nxt = nxt_ref[i]

        @pl.when(nxt >= 0)
        def _():
            copy(nxt, 1 - slot).start()

        slot_ref[0] = 1 - slot

    @pl.when(i < nu)
    def _():
        y_ref[...] = _dot(h_ref[...], wdb_ref[...])

    @pl.when(i >= nu)
    def _():
        y_ref[...] = jnp.zeros_like(y_ref)


def expert_down(h, blk_e, n_used, nxt_e, w_down, layer):
    S, Fd = h.shape
    _, E, _, D = w_down.shape
    tm = MOE_TM
    nb = S // tm
    kern = functools.partial(_expert_down_kernel, layer=layer)
    return pl.pallas_call(
        kern,
        grid_spec=pltpu.PrefetchScalarGridSpec(
            num_scalar_prefetch=3,
            grid=(nb,),
            in_specs=[
                pl.BlockSpec((tm, Fd), lambda i, be, nu, nx: (jnp.minimum(i, nu[0] - 1), 0)),
                pl.BlockSpec(memory_space=pl.ANY),
            ],
            out_specs=pl.BlockSpec((tm, D), lambda i, be, nu, nx: (i, 0)),
            scratch_shapes=[pltpu.VMEM((2, Fd, D), F32), pltpu.VMEM((Fd, D), BF16), pltpu.SMEM((1,), I32),
                            pltpu.SemaphoreType.DMA((2,))],
        ),
        out_shape=jax.ShapeDtypeStruct((S, D), F32),
        compiler_params=_cparams(("arbitrary",), 56),
        name="expert_down",
    )(blk_e, n_used, nxt_e, h, w_down)


def _combine_kernel(dcur_ref, dnext_ref, y_ref, w_ref, sh_ref, x1_ref, g_ref, mod_ref, o_ref, buf_ref, sem, *, tb, nb):
    i = pl.program_id(0)
    cur = lax.rem(i, 2)

    def fetch(dref, half):
        def issue(r, c):
            for k in range(TOP_K):
                d = dref[0, k * tb + r]
                pltpu.make_async_copy(y_ref.at[pl.ds(d, 1)], buf_ref.at[half, k, pl.ds(r, 1)], sem.at[half]).start()
            return c

        lax.fori_loop(0, tb, issue, 0, unroll=2)

    @pl.when(i == 0)
    def _():
        fetch(dcur_ref, 0)

    @pl.when(i + 1 < nb)
    def _():
        fetch(dnext_ref, 1 - cur)

    def drain(r, c):
        for k in range(TOP_K):
            pltpu.make_async_copy(y_ref.at[pl.ds(0, 1)], buf_ref.at[cur, k, pl.ds(0, 1)], sem.at[cur]).wait()
        return c

    lax.fori_loop(0, tb, drain, 0, unroll=2)
    w = w_ref[...]
    acc = sh_ref[...]
    for k in range(TOP_K):
        acc = acc + w[:, k:k + 1] * buf_ref[cur, k]
    o_ref[...] = x1_ref[...] + mod_ref[5:6, :] * (_rms(acc) * g_ref[3:4, :])


def combine(y_sorted, dest, w, shared, x1, g, mods, n_ctx, t_lat):
    N, D = x1.shape
    tb = 64
    nb = N // tb
    dest3 = _block_major(dest, tb)
    kern = functools.partial(_combine_kernel, tb=tb, nb=nb)
    return pl.pallas_call(
        kern,
        grid=(nb,),
        in_specs=[
            pl.BlockSpec((None, 1, TOP_K * tb), lambda i: (i, 0, 0), memory_space=pltpu.SMEM),
            pl.BlockSpec((None, 1, TOP_K * tb), lambda i: (jnp.minimum(i + 1, nb - 1), 0, 0), memory_space=pltpu.SMEM),
            pl.BlockSpec(memory_space=pl.ANY),
            pl.BlockSpec((tb, TOP_K), lambda i: (i, 0)),
            pl.BlockSpec((tb, D), lambda i: (i, 0)),
            pl.BlockSpec((tb, D), lambda i: (i, 0)),
            pl.BlockSpec((4, D), lambda i: (0, 0)),
            pl.BlockSpec((None, 6, D), lambda i: (_mod_row_index(i, tb, n_ctx, t_lat), 0, 0)),
        ],
        out_specs=pl.BlockSpec((tb, D), lambda i: (i, 0)),
        out_shape=jax.ShapeDtypeStruct((N, D), F32),
        scratch_shapes=[pltpu.VMEM((2, TOP_K, tb, D), F32), pltpu.SemaphoreType.DMA((2,))],
        compiler_params=_cparams(("arbitrary",), 40),
        name="combine",
    )(dest3, dest3, y_sorted, w, shared, x1, g, mods)


def moe_ffn(h2, hp, x1, p, g, mods, n_ctx, t_lat, layer):
    N, D = x1.shape
    tm = MOE_TM
    idx, w, rank, cnt = router(h2, p['router_wt'], p['router_b'])
    counts = cnt[:, 0].astype(I32)
    padded = (counts + tm - 1) // tm * tm
    pend = jnp.cumsum(padded)
    pstart = pend - padded
    n_blocks = (N * TOP_K + N_EXPERTS * (tm - 1) + tm - 1) // tm
    n_used = (pend[-1] // tm).astype(I32).reshape(1)
    blk_start = jnp.arange(n_blocks, dtype=I32) * tm
    blk_e = jnp.minimum(jnp.sum((pend[None, :] <= blk_start[:, None]).astype(I32), axis=1), N_EXPERTS - 1)
    hit = idx[:, :, None] == jnp.arange(N_EXPERTS, dtype=I32)[None, None, :]
    dest = jnp.sum(jnp.where(hit, pstart[None, None, :], 0), axis=-1) + rank
    after = jnp.sum(jnp.where(blk_e[:, None] == jnp.arange(N_EXPERTS, dtype=I32)[None, :], pend[None, :], 0),
                    axis=1) // tm
    nxt_e = jnp.where(after < n_used[0], blk_e[jnp.minimum(after, n_blocks - 1)], -1).astype(I32)
    n_fill = n_blocks * tm - N * TOP_K
    pad = padded - counts
    pad_end = jnp.cumsum(pad)
    q = jnp.arange(n_fill, dtype=I32)
    owner = jnp.sum((pad_end[None, :] <= q[:, None]).astype(I32), axis=1)
    oh = owner[:, None] == jnp.arange(N_EXPERTS, dtype=I32)[None, :]
    base = jnp.sum(jnp.where(oh, (pstart + counts - (pad_end - pad))[None, :], 0), axis=1)
    fill = jnp.where(owner < N_EXPERTS, base + q, pend[-1] + (q - pad_end[-1])).astype(I32)
    xs = dispatch(hp, dest, fill)
    hid = expert_up(xs, blk_e, n_used, nxt_e, p['exp_w_gate'], p['exp_w_up'], layer)
    ys = expert_down(hid, blk_e, n_used, nxt_e, p['exp_w_down'], layer)
    sh_hid = swiglu(h2, p['sh_w_gate'], p['sh_w_up'])
    shared = matmul(sh_hid, p['sh_w_down'])
    return combine(ys, dest, w.T, shared, x1, g, mods, n_ctx, t_lat)


def _axial_rope_tables(n):
    rows = n // GRID_W
    t = jnp.arange(rows * GRID_W)
    row = (t // GRID_W).astype(F32)
    col = (t % GRID_W).astype(F32)
    n_freq = LANES // 4
    inv = ROPE_BASE ** (-jnp.arange(n_freq, dtype=F32) / n_freq)
    ang = jnp.concatenate([row[:, None] * inv, col[:, None] * inv], axis=-1)
    cos, sin = jnp.cos(ang), jnp.sin(ang)
    return jnp.concatenate([cos, cos], axis=-1), jnp.concatenate([-sin, sin], axis=-1)


def _layer_params(l, w_in, conv_w, conv_b, lru_wa, lru_ba, lru_wx, lru_bx, lru_lam, diff_lam, diff_subln,
                  gla_w, gla_b, gla_norm, w_out, router_w, router_b, exp_w_gate, exp_w_up, exp_w_down,
                  sh_w_gate, sh_w_up, sh_w_down):
    d_lru = conv_w.shape[-1]
    n_main = w_in.shape[-1] - 2 * GLA_RANK
    dk = LANES
    wa, wx = lru_wa[l], lru_wx[l]
    w_gate = jnp.concatenate([wa[0], wx[0], wa[1], wx[1]], axis=-1).astype(BF16)
    ba = lru_ba[l].reshape(2, H_LRU, 1, -1)
    bx = lru_bx[l].reshape(2, H_LRU, 1, -1)
    b_gate = jnp.concatenate([ba[0], bx[0], ba[1], bx[1]], axis=-1)
    gw = gla_w[l]
    w_glog = jnp.zeros((H_GLA, LANES, 2 * dk), F32)
    for d in range(2):
        blk = gw[d].reshape(GLA_RANK, H_GLA, dk).transpose(1, 0, 2)
        w_glog = w_glog.at[:, d * GLA_RANK:(d + 1) * GLA_RANK, d * dk:(d + 1) * dk].set(blk)
    gb = gla_b[l].reshape(2, H_GLA, 1, dk)
    b_glog = jnp.concatenate([gb[0], gb[1]], axis=-1)
    w_lr = jnp.zeros((w_in.shape[1], LANES), F32).at[:, :2 * GLA_RANK].set(w_in[l][:, n_main:])
    return {
        'w_main': w_in[l][:, :n_main].astype(BF16), 'w_lr': w_lr.astype(BF16),
        'conv_w': conv_w[l], 'conv_b': conv_b[l].reshape(1, d_lru),
        'w_gate': w_gate, 'b_gate': b_gate, 'lru_lam': lru_lam[l],
        'diff_lam': diff_lam[l], 'diff_subln': diff_subln[l].reshape(1, -1),
        'w_glog': w_glog.astype(BF16), 'b_glog': b_glog, 'gla_norm': gla_norm[l].reshape(1, -1),
        'w_out': w_out[l].astype(BF16),
        'router_wt': router_w[l].T.astype(BF16),
        'router_b': jnp.broadcast_to(router_b[l][:, None], (N_EXPERTS, LANES)),
        'exp_w_gate': exp_w_gate, 'exp_w_up': exp_w_up, 'exp_w_down': exp_w_down,
        'sh_w_gate': sh_w_gate[l].astype(BF16), 'sh_w_up': sh_w_up[l].astype(BF16),
        'sh_w_down': sh_w_down[l].astype(BF16),
    }


def kernel(x_prompt, x_sample, cache_k, cache_v, state_lru, state_gla, c, c_ctx, mod_w, mod_b, norm_g, w_in,
           conv_w, conv_b, lru_wa, lru_ba, lru_wx, lru_bx, lru_lam, diff_lam, diff_subln, gla_w, gla_b,
           gla_norm, w_out, router_w, router_b, exp_w_gate, exp_w_up, exp_w_down, sh_w_gate, sh_w_up,
           sh_w_down):
    Bc, Tc, D = x_prompt.shape
    Bl, Tl, _ = x_sample.shape
    L = mod_w.shape[0]
    n_ctx, n_lat = Bc * Tc, Bl * Tl
    d_lru = conv_w.shape[-1]
    d_diff = H_DIFF * 2 * LANES
    d_gk = H_GLA * LANES
    d_gv = H_GLA * 2 * LANES
    col_qb = 2 * d_lru
    col_qc = col_qb + 3 * d_diff
    col_kc = col_qc + d_gk
    col_vc = col_kc + d_gk
    col_gc = col_vc + d_gv

    x = jnp.concatenate([x_prompt.reshape(n_ctx, D), x_sample.reshape(n_lat, D)], axis=0)
    n_rows = 16 * ((1 + Bl + 15) // 16)
    c_rows = jnp.zeros((n_rows, D), F32).at[0].set(c_ctx).at[1:1 + Bl].set(c)
    mods_all = modulation(c_rows, mod_w, mod_b).reshape(L, n_rows, 6, D)
    rope = _axial_rope_tables(Tl)
    cache_k2 = cache_k.reshape(*cache_k.shape[:3], -1)
    cache_v2 = cache_v.reshape(*cache_v.shape[:3], -1)
    zeros_lru = jnp.zeros((Bc, 2, d_lru), F32)
    zeros_gla = jnp.zeros((Bc, 2, H_GLA, LANES, 2 * LANES), F32)

    ks, vs, lrus, glas = [], [], [], []
    for l in range(L):
        p = _layer_params(l, w_in, conv_w, conv_b, lru_wa, lru_ba, lru_wx, lru_bx, lru_lam, diff_lam,
                          diff_subln, gla_w, gla_b, gla_norm, w_out, router_w, router_b, exp_w_gate,
                          exp_w_up, exp_w_down, sh_w_gate, sh_w_up, sh_w_down)
        lam_init = 0.8 - 0.6 * math.exp(-0.3 * l)
        mods = mods_all[l]
        g = norm_g[l]
        h = norm_mod(x, g[0:1], mods, n_ctx, Tl)
        z = matmul(h, p['w_main'])
        z_lr = matmul(h, p['w_lr'])
        ycat = jnp.zeros((n_ctx + n_lat, D), BF16)
        ycat, lru_c = lru_mixer(z, ycat, 0, Bc, Tc, p['conv_w'], p['conv_b'], p['w_gate'], p['b_gate'],
                                p['lru_lam'], zeros_lru)
        ycat, _ = lru_mixer(z, ycat, n_ctx, Bl, Tl, p['conv_w'], p['conv_b'], p['w_gate'], p['b_gate'],
                            p['lru_lam'], state_lru[:, l])
        ycat = diff_attention(z, ycat, d_lru, 0, Bc, Tc, col_qb, None, None, l, None, p['diff_lam'],
                              p['diff_subln'], lam_init)
        ycat = diff_attention(z, ycat, d_lru, n_ctx, Bl, Tl, col_qb, cache_k2, cache_v2, l, rope, p['diff_lam'],
                              p['diff_subln'], lam_init)
        ycat, gla_c = gla_mixer(z, z_lr, ycat, d_lru + d_diff, 0, Bc, Tc, col_qc, col_kc, col_vc, col_gc,
                                p['w_glog'], p['b_glog'], p['gla_norm'], zeros_gla)
        ycat, _ = gla_mixer(z, z_lr, ycat, d_lru + d_diff, n_ctx, Bl, Tl, col_qc, col_kc, col_vc, col_gc,
                            p['w_glog'], p['b_glog'], p['gla_norm'], state_gla[:, l])
        y = matmul(ycat, p['w_out'])
        x1, h2, hp = post_mix(x, y, g, mods, n_ctx, Tl)
        x = moe_ffn(h2, hp, x1, p, g, mods, n_ctx, Tl, l)
        ks.append(z[:n_ctx, col_qb + d_diff:col_qb + 2 * d_diff].reshape(Bc, Tc, H_DIFF, 2 * LANES))
        vs.append(z[:n_ctx, col_qb + 2 * d_diff:col_qb + 3 * d_diff].reshape(Bc, Tc, H_DIFF, 2 * LANES))
        lrus.append(lru_c)
        glas.append(gla_c)
    yp = x[:n_ctx].reshape(Bc, Tc, D)
    ys = x[n_ctx:].reshape(Bl, Tl, D)
    return (yp, ys, jnp.stack(ks, axis=1), jnp.stack(vs, axis=1), jnp.stack(lrus, axis=1), jnp.stack(glas, axis=1))
```

```python
import functools
import math

import jax
import jax.numpy as jnp
from jax import lax
from jax.experimental import pallas as pl
from jax.experimental.pallas import tpu as pltpu

F32 = jnp.float32
BF16 = jnp.bfloat16
U32 = jnp.uint32
I32 = jnp.int32

EPS = 1e-6
GRID_W = 64
H_LRU = 8
CONV_W = 4
CONV_LEFT = 2
LRU_C = 8.0
H_DIFF = 8
ROPE_BASE = 10000.0
H_GLA = 4
GLA_RANK = 16
GLA_TAU = 16.0
GLA_CHUNK = 64
N_EXPERTS = 64
TOP_K = 8
N_GROUPS = 8
TOPK_GROUPS = 4
ROUTED_SCALE = 2.5

V7X_VMEM_BYTES = 64 * 1024 * 1024
LANES = 128
SUBLANES = 8
MIB = 1024 * 1024


def _cparams(semantics, vmem_mib):
    assert vmem_mib * MIB < V7X_VMEM_BYTES
    return pltpu.CompilerParams(dimension_semantics=semantics, vmem_limit_bytes=vmem_mib * MIB)


def _rms(x):
    return x * lax.rsqrt(jnp.mean(x * x, axis=-1, keepdims=True) + EPS)


def _sigmoid(x):
    return 1.0 / (1.0 + jnp.exp(-x))


def _silu(x):
    return x * _sigmoid(x)


def _dot(a, b):
    return jnp.dot(a, b, preferred_element_type=F32)


def _dot_nt(a, b):
    return lax.dot_general(a, b, (((1,), (1,)), ((), ())), preferred_element_type=F32)


def _dot_tn(a, b):
    return lax.dot_general(a, b, (((0,), (0,)), ((), ())), preferred_element_type=F32)


def _mod_kernel(c_ref, w_ref, b_ref, o_ref):
    s = _silu(c_ref[...]).astype(BF16)
    o_ref[...] = _dot(s, w_ref[...].astype(BF16)) + b_ref[...]


def modulation(c_rows, mod_w, mod_b):
    L, D, N6 = mod_w.shape
    R = c_rows.shape[0]
    tn = 512
    return pl.pallas_call(
        _mod_kernel,
        grid=(L, N6 // tn),
        in_specs=[
            pl.BlockSpec((R, D), lambda l, j: (0, 0)),
            pl.BlockSpec((None, D, tn), lambda l, j: (l, 0, j)),
            pl.BlockSpec((None, 1, tn), lambda l, j: (l, 0, j)),
        ],
        out_specs=pl.BlockSpec((None, R, tn), lambda l, j: (l, 0, j)),
        out_shape=jax.ShapeDtypeStruct((L, R, N6), F32),
        compiler_params=_cparams(("arbitrary", "arbitrary"), 40),
        name="modulation",
    )(c_rows, mod_w, mod_b.reshape(L, 1, N6))


def _mod_row_index(i, tm, n_ctx, t_lat):
    r0 = i * tm
    return jnp.where(r0 < n_ctx, 0, 1 + (r0 - n_ctx) // t_lat)


def _norm_mod_kernel(x_ref, g_ref, mod_ref, o_ref):
    y = _rms(x_ref[...]) * g_ref[...]
    o_ref[...] = (y * (1.0 + mod_ref[1:2, :]) + mod_ref[0:1, :]).astype(o_ref.dtype)


def norm_mod(x, g, mods, n_ctx, t_lat):
    N, D = x.shape
    tm = 256
    return pl.pallas_call(
        _norm_mod_kernel,
        grid=(N // tm,),
        in_specs=[
            pl.BlockSpec((tm, D), lambda i: (i, 0)),
            pl.BlockSpec((1, D), lambda i: (0, 0)),
            pl.BlockSpec((None, 6, D), lambda i: (_mod_row_index(i, tm, n_ctx, t_lat), 0, 0)),
        ],
        out_specs=pl.BlockSpec((tm, D), lambda i: (i, 0)),
        out_shape=jax.ShapeDtypeStruct((N, D), BF16),
        compiler_params=_cparams(("arbitrary",), 32),
        name="norm_mod",
    )(x, g, mods)


def _mm_kernel(x_ref, w_ref, o_ref):
    o_ref[...] = _dot(x_ref[...], w_ref[...]).astype(o_ref.dtype)


def matmul(x, w, out_dtype=F32, tm=1024, tn=512):
    M, K = x.shape
    N = w.shape[1]
    tm = min(tm, M)
    tn = min(tn, N)
    assert M % tm == 0 and N % tn == 0
    return pl.pallas_call(
        _mm_kernel,
        grid=(M // tm, N // tn),
        in_specs=[pl.BlockSpec((tm, K), lambda i, j: (i, 0)), pl.BlockSpec((K, tn), lambda i, j: (0, j))],
        out_specs=pl.BlockSpec((tm, tn), lambda i, j: (i, j)),
        out_shape=jax.ShapeDtypeStruct((M, N), out_dtype),
        compiler_params=_cparams(("arbitrary", "arbitrary"), 48),
        name="matmul",
    )(x, w)


def _swiglu_kernel(x_ref, wg_ref, wu_ref, o_ref):
    x = x_ref[...]
    o_ref[...] = (_silu(_dot(x, wg_ref[...])) * _dot(x, wu_ref[...])).astype(o_ref.dtype)


def swiglu(x, wg, wu, tm=1024, tn=256):
    M, K = x.shape
    N = wg.shape[1]
    tm = min(tm, M)
    return pl.pallas_call(
        _swiglu_kernel,
        grid=(M // tm, N // tn),
        in_specs=[
            pl.BlockSpec((tm, K), lambda i, j: (i, 0)),
            pl.BlockSpec((K, tn), lambda i, j: (0, j)),
            pl.BlockSpec((K, tn), lambda i, j: (0, j)),
        ],
        out_specs=pl.BlockSpec((tm, tn), lambda i, j: (i, j)),
        out_shape=jax.ShapeDtypeStruct((M, N), BF16),
        compiler_params=_cparams(("arbitrary", "arbitrary"), 48),
        name="swiglu",
    )(x, wg, wu)


def _lru_kernel(xa_ref, ga_ref, cw_ref, cb_ref, wg_ref, bg_ref, lam_ref, h0_ref, ybuf_ref, y_ref, hl_ref,
                pad_ref, af_ref, uf_ref, ab_ref, ub_ref, *, T, CW):
    del ybuf_ref
    pad_ref[0:SUBLANES, :] = jnp.zeros((SUBLANES, CW), F32)
    pad_ref[SUBLANES + T:2 * SUBLANES + T, :] = jnp.zeros((SUBLANES, CW), F32)
    pad_ref[SUBLANES:SUBLANES + T, :] = xa_ref[...]
    xc = cb_ref[...] + jnp.zeros((T, CW), F32)
    for j in range(CONV_W):
        o = SUBLANES - CONV_LEFT + j
        xc = xc + cw_ref[j:j + 1, :] * pad_ref[o:o + T, :]
    lam = lam_ref[...]
    nl = -lam
    softplus = jnp.maximum(nl, 0.0) + jnp.log(1.0 + jnp.exp(-jnp.abs(nl)))
    bs = LANES
    for hh in range(CW // bs):
        cs = slice(hh * bs, (hh + 1) * bs)
        xb = xc[:, cs]
        gts = _dot(xb.astype(BF16), wg_ref[hh]) + bg_ref[hh]
        for d, (a_ref, u_ref) in enumerate(((af_ref, uf_ref), (ab_ref, ub_ref))):
            r = _sigmoid(gts[:, (2 * d) * bs:(2 * d + 1) * bs])
            ig = _sigmoid(gts[:, (2 * d + 1) * bs:(2 * d + 2) * bs])
            log_a = (-LRU_C) * r * softplus[d:d + 1, cs]
            a = jnp.exp(log_a)
            a_ref[:, cs] = a
            u_ref[:, cs] = jnp.sqrt(1.0 - a * a) * (ig * xb)

    def step(t, carry):
        hf, hb = carry
        hf = af_ref[pl.ds(t, 1), :] * hf + uf_ref[pl.ds(t, 1), :]
        uf_ref[pl.ds(t, 1), :] = hf
        tb = T - 1 - t
        hb = ab_ref[pl.ds(tb, 1), :] * hb + ub_ref[pl.ds(tb, 1), :]
        ub_ref[pl.ds(tb, 1), :] = hb
        return hf, hb

    hf, hb = lax.fori_loop(0, T, step, (h0_ref[0:1, :], h0_ref[1:2, :]), unroll=8)
    hl_ref[0:1, :] = hf
    hl_ref[1:2, :] = hb
    y_ref[...] = ((uf_ref[...] + ub_ref[...]) * jax.nn.gelu(ga_ref[...], approximate=True)).astype(y_ref.dtype)


def lru_mixer(z, ybuf, row0, B, T, conv_w, conv_b, w_gate, b_gate, lam, h0):
    d_lru = conv_w.shape[1]
    CW = 512
    nj = d_lru // CW
    rb0 = row0 // T
    kern = functools.partial(_lru_kernel, T=T, CW=CW)
    return pl.pallas_call(
        kern,
        grid=(B, nj),
        in_specs=[
            pl.BlockSpec((T, CW), lambda b, j: (rb0 + b, j)),
            pl.BlockSpec((T, CW), lambda b, j: (rb0 + b, nj + j)),
            pl.BlockSpec((CONV_W, CW), lambda b, j: (0, j)),
            pl.BlockSpec((1, CW), lambda b, j: (0, j)),
            pl.BlockSpec((CW // LANES, LANES, 4 * LANES), lambda b, j: (j, 0, 0)),
            pl.BlockSpec((CW // LANES, 1, 4 * LANES), lambda b, j: (j, 0, 0)),
            pl.BlockSpec((2, CW), lambda b, j: (0, j)),
            pl.BlockSpec((None, 2, CW), lambda b, j: (b, 0, j)),
            pl.BlockSpec(memory_space=pl.ANY),
        ],
        out_specs=[
            pl.BlockSpec((T, CW), lambda b, j: (rb0 + b, j)),
            pl.BlockSpec((None, 2, CW), lambda b, j: (b, 0, j)),
        ],
        out_shape=[jax.ShapeDtypeStruct(ybuf.shape, ybuf.dtype), jax.ShapeDtypeStruct((B, 2, d_lru), F32)],
        scratch_shapes=[pltpu.VMEM((T + 2 * SUBLANES, CW), F32)] + [pltpu.VMEM((T, CW), F32)] * 4,
        input_output_aliases={8: 0},
        compiler_params=_cparams(("arbitrary", "arbitrary"), 40),
        name="lru_mixer",
    )(z, z, conv_w, conv_b, w_gate, b_gate, lam, h0, ybuf)


def _rope(x, cos2, sin2):
    return x * cos2 + pltpu.roll(x, LANES // 2, axis=1) * sin2


def _attn_kernel(*refs, T, P, tq, use_rope, lam_init):
    it = iter(refs)
    q_ref, k_ref, v_ref = next(it), next(it), next(it)
    ck_ref = cv_ref = cos_ref = sin_ref = None
    if P:
        ck_ref, cv_ref = next(it), next(it)
    if use_rope:
        cos_ref, sin_ref = next(it), next(it)
    dl_ref, sub_ref, _, o_ref, kall_ref, vall_ref = next(it), next(it), next(it), next(it), next(it), next(it)
    dh = LANES
    qi = pl.program_id(2)

    @pl.when(qi == 0)
    def _():
        if P:
            kall_ref[0:P, :] = ck_ref[...].astype(BF16)
            vall_ref[0:P, :] = cv_ref[...].astype(BF16)
        for m in range(2):
            km = k_ref[:, m * dh:(m + 1) * dh]
            if use_rope:
                km = _rope(km, cos_ref[...], sin_ref[...])
            kall_ref[P:P + T, m * dh:(m + 1) * dh] = km.astype(BF16)
        vall_ref[P:P + T, :] = v_ref[...].astype(BF16)

    dl = dl_ref[...]
    lam = (jnp.exp(jnp.sum(dl[0:1, :] * dl[1:2, :], axis=-1, keepdims=True))
           - jnp.exp(jnp.sum(dl[2:3, :] * dl[3:4, :], axis=-1, keepdims=True)) + lam_init)
    qscale = (dh ** -0.5) * math.log2(math.e)
    pv, invs = [], []
    for m in range(2):
        qm = q_ref[:, m * dh:(m + 1) * dh]
        if use_rope:
            r0 = pl.multiple_of(qi * tq, tq)
            qm = _rope(qm, cos_ref[pl.ds(r0, tq), :], sin_ref[pl.ds(r0, tq), :])
        s2 = _dot_nt((qm * qscale).astype(BF16), kall_ref[:, m * dh:(m + 1) * dh])
        e = jnp.exp2(s2 - jnp.max(s2, axis=-1, keepdims=True))
        invs.append(1.0 / jnp.sum(e, axis=-1, keepdims=True))
        pv.append(_dot(e.astype(BF16), vall_ref[...]))
    o = pv[0] * invs[0] - pv[1] * (lam * invs[1])
    o_ref[...] = (_rms(o) * sub_ref[...] * (1.0 - lam_init)).astype(o_ref.dtype)


def diff_attention(z, ybuf, col_y, row0, B, T, col_q, cache_k, cache_v, layer, rope, diff_lam, subln, lam_init):
    H = H_DIFF
    dw = 2 * LANES
    P = 0 if cache_k is None else cache_k.shape[2]
    tq = min(T, 1024)
    nq = T // tq
    rbq = row0 // tq
    rbk = row0 // T
    cq = col_q // dw
    kern = functools.partial(_attn_kernel, T=T, P=P, tq=tq, use_rope=rope is not None, lam_init=lam_init)
    in_specs = [
        pl.BlockSpec((tq, dw), lambda b, h, i: (rbq + b * nq + i, cq + h)),
        pl.BlockSpec((T, dw), lambda b, h, i: (rbk + b, cq + H + h)),
        pl.BlockSpec((T, dw), lambda b, h, i: (rbk + b, cq + 2 * H + h)),
    ]
    args = [z, z, z]
    if P:
        in_specs += [pl.BlockSpec((None, None, P, dw), lambda b, h, i: (b, layer, 0, h))] * 2
        args += [cache_k, cache_v]
    if rope is not None:
        in_specs += [pl.BlockSpec((T, LANES), lambda b, h, i: (0, 0))] * 2
        args += [rope[0], rope[1]]
    in_specs += [pl.BlockSpec((4, LANES), lambda b, h, i: (0, 0)), pl.BlockSpec((1, dw), lambda b, h, i: (0, 0)),
                 pl.BlockSpec(memory_space=pl.ANY)]
    args += [diff_lam, subln, ybuf]
    cy = col_y // dw
    return pl.pallas_call(
        kern,
        grid=(B, H, nq),
        in_specs=in_specs,
        out_specs=pl.BlockSpec((tq, dw), lambda b, h, i: (rbq + b * nq + i, cy + h)),
        out_shape=jax.ShapeDtypeStruct(ybuf.shape, ybuf.dtype),
        scratch_shapes=[pltpu.VMEM((P + T, dw), BF16), pltpu.VMEM((P + T, dw), BF16)],
        input_output_aliases={len(args) - 1: 0},
        compiler_params=_cparams(("arbitrary", "arbitrary", "arbitrary"), 40),
        name="diff_attention",
    )(*args)


def _gla_kernel(q_ref, k_ref, v_ref, gc_ref, lr_ref, wgl_ref, bgl_ref, gn_ref, s0_ref, ybuf_ref, y_ref, so_ref,
                glog_ref, qt_ref, ec_ref, kv_ref, o_ref, s_ref, *, T):
    del ybuf_ref
    C = GLA_CHUNK
    dk = LANES
    nc = T // C
    gl = _dot(lr_ref[...].astype(BF16), wgl_ref[...]) + bgl_ref[...]
    glog_ref[...] = (jnp.minimum(gl, 0.0) - jnp.log(1.0 + jnp.exp(-jnp.abs(gl)))) * (1.0 / GLA_TAU)
    row = lax.broadcasted_iota(I32, (C, C), 0)
    col = lax.broadcasted_iota(I32, (C, C), 1)
    qscale = dk ** -0.5
    ones = jnp.ones((C, dk), BF16)

    def prepare(c, carry):
        r0 = pl.multiple_of(c * C, C)
        q = q_ref[pl.ds(r0, C), :] * qscale
        k = k_ref[pl.ds(r0, C), :]
        v = v_ref[pl.ds(r0, C), :].astype(BF16)
        for d in range(2):
            keep = (row >= col) if d == 0 else (row <= col)
            tri = jnp.where(keep, 1.0, 0.0).astype(BF16)
            g = glog_ref[pl.ds(r0, C), d * dk:(d + 1) * dk]
            g_hi = g.astype(BF16)
            g_lo = (g - g_hi.astype(F32)).astype(BF16)
            b = _dot(tri, g_hi) + _dot(tri, g_lo)
            b_end = b[C - 1:C, :] if d == 0 else b[0:1, :]
            qt = (q * jnp.exp(b)).astype(BF16)
            kt = (k * jnp.exp(-b)).astype(BF16)
            att = jnp.where(keep, _dot_nt(qt, kt), 0.0)
            qt_ref[d, pl.ds(r0, C), :] = qt
            o_ref[d, pl.ds(r0, C), :] = _dot(att.astype(BF16), v)
            kd = (k * jnp.exp(b_end - b)).astype(BF16)
            bcol = _dot_tn(g_hi, ones) + _dot_tn(g_lo, ones)
            ec_ref[d, c] = jnp.exp(bcol)
            kv_ref[d, c] = _dot_tn(kd, v)
        return carry

    lax.fori_loop(0, nc, prepare, 0, unroll=4)
    s_ref[...] = s0_ref[...]

    def advance(ci, carry):
        for d in range(2):
            c = ci if d == 0 else nc - 1 - ci
            r0 = pl.multiple_of(c * C, C)
            S = s_ref[d]
            o_ref[d, pl.ds(r0, C), :] = o_ref[d, pl.ds(r0, C), :] + _dot(qt_ref[d, pl.ds(r0, C), :], S.astype(BF16))
            ec = ec_ref[d, c]
            s_ref[d] = jnp.concatenate([ec, ec], axis=1) * S + kv_ref[d, c]
        return carry

    lax.fori_loop(0, nc, advance, 0)
    so_ref[...] = s_ref[...]
    oc = o_ref[0] + o_ref[1]
    y_ref[...] = (_rms(oc) * gn_ref[...] * _silu(gc_ref[...])).astype(y_ref.dtype)


def gla_mixer(z, z_lr, ybuf, col_y, row0, B, T, col_q, col_k, col_v, col_g, w_glog, b_glog, gnorm, s0):
    H = H_GLA
    dk, dv = LANES, 2 * LANES
    rb = row0 // T
    kern = functools.partial(_gla_kernel, T=T)
    return pl.pallas_call(
        kern,
        grid=(B, H),
        in_specs=[
            pl.BlockSpec((T, dk), lambda b, h: (rb + b, col_q // dk + h)),
            pl.BlockSpec((T, dk), lambda b, h: (rb + b, col_k // dk + h)),
            pl.BlockSpec((T, dv), lambda b, h: (rb + b, col_v // dv + h)),
            pl.BlockSpec((T, dv), lambda b, h: (rb + b, col_g // dv + h)),
            pl.BlockSpec((T, LANES), lambda b, h: (rb + b, 0)),
            pl.BlockSpec((None, LANES, 2 * dk), lambda b, h: (h, 0, 0)),
            pl.BlockSpec((None, 1, 2 * dk), lambda b, h: (h, 0, 0)),
            pl.BlockSpec((1, dv), lambda b, h: (0, 0)),
            pl.BlockSpec((None, 2, None, dk, dv), lambda b, h: (b, 0, h, 0, 0)),
            pl.BlockSpec(memory_space=pl.ANY),
        ],
        out_specs=[
            pl.BlockSpec((T, dv), lambda b, h: (rb + b, col_y // dv + h)),
            pl.BlockSpec((None, 2, None, dk, dv), lambda b, h: (b, 0, h, 0, 0)),
        ],
        out_shape=[jax.ShapeDtypeStruct(ybuf.shape, ybuf.dtype), jax.ShapeDtypeStruct((B, 2, H, dk, dv), F32)],
        scratch_shapes=[pltpu.VMEM((T, 2 * dk), F32), pltpu.VMEM((2, T, dk), BF16),
                        pltpu.VMEM((2, T // GLA_CHUNK, dk, dk), F32), pltpu.VMEM((2, T // GLA_CHUNK, dk, dv), F32),
                        pltpu.VMEM((2, T, dv), F32), pltpu.VMEM((2, dk, dv), F32)],
        input_output_aliases={9: 0},
        compiler_params=_cparams(("arbitrary", "arbitrary"), 32),
        name="gla_mixer",
    )(z, z, z, z, z_lr, w_glog, b_glog, gnorm, s0, ybuf)


def _post_mix_kernel(x_ref, y_ref, g_ref, mod_ref, x1_ref, h2_ref, hp_ref):
    D = x_ref.shape[1]
    x1 = x_ref[...] + mod_ref[2:3, :] * (_rms(y_ref[...]) * g_ref[1:2, :])
    x1_ref[...] = x1
    h = (_rms(x1) * g_ref[2:3, :]) * (1.0 + mod_ref[4:5, :]) + mod_ref[3:4, :]
    hb = h.astype(BF16)
    h2_ref[...] = hb
    u = pltpu.bitcast(hb.astype(F32), U32)
    hp_ref[...] = (u[:, :D // 2] >> 16) | (u[:, D // 2:] & jnp.uint32(0xFFFF0000))


def post_mix(x, y, g, mods, n_ctx, t_lat):
    N, D = x.shape
    tm = 256
    return pl.pallas_call(
        _post_mix_kernel,
        grid=(N // tm,),
        in_specs=[
            pl.BlockSpec((tm, D), lambda i: (i, 0)),
            pl.BlockSpec((tm, D), lambda i: (i, 0)),
            pl.BlockSpec((4, D), lambda i: (0, 0)),
            pl.BlockSpec((None, 6, D), lambda i: (_mod_row_index(i, tm, n_ctx, t_lat), 0, 0)),
        ],
        out_specs=[
            pl.BlockSpec((tm, D), lambda i: (i, 0)),
            pl.BlockSpec((tm, D), lambda i: (i, 0)),
            pl.BlockSpec((tm, D // 2), lambda i: (i, 0)),
        ],
        out_shape=[jax.ShapeDtypeStruct((N, D), F32), jax.ShapeDtypeStruct((N, D), BF16),
                   jax.ShapeDtypeStruct((N, D // 2), U32)],
        compiler_params=_cparams(("arbitrary",), 48),
        name="post_mix",
    )(x, y, g, mods)


def _router_kernel(h_ref, rwt_ref, rb_ref, upper_ref, idx_ref, w_ref, rank_ref, cnt_ref, carry_ref, *, tm):
    i = pl.program_id(0)
    per = N_EXPERTS // N_GROUPS

    @pl.when(i == 0)
    def _():
        carry_ref[...] = jnp.zeros_like(carry_ref)

    logits = _dot_nt(rwt_ref[...], h_ref[...])
    s_all = _sigmoid(logits)
    sel_all = s_all + rb_ref[...][:, 0:1]
    sub = lax.broadcasted_iota(I32, (per, tm), 0)
    neg = -jnp.inf
    s_g, sel_g, gscore = [], [], []
    for g in range(N_GROUPS):
        blk = sel_all[g * per:(g + 1) * per, :]
        s_g.append(s_all[g * per:(g + 1) * per, :])
        sel_g.append(blk)
        m1 = jnp.max(blk, axis=0, keepdims=True)
        first = jnp.min(jnp.where(blk == m1, sub, per), axis=0, keepdims=True)
        m2 = jnp.max(jnp.where(sub == first, neg, blk), axis=0, keepdims=True)
        gscore.append(m1 + m2)
    for g in range(N_GROUPS):
        beat = jnp.zeros((1, tm), F32)
        for g2 in range(N_GROUPS):
            if g2 == g:
                continue
            wins = (gscore[g2] > gscore[g]) | ((gscore[g2] == gscore[g]) & (g2 < g))
            beat = beat + jnp.where(wins, 1.0, 0.0)
        sel_g[g] = jnp.where(beat < TOPK_GROUPS, sel_g[g], neg)
    eidx = [sub + g * per for g in range(N_GROUPS)]
    onehot = [jnp.zeros((per, tm), F32) for _ in range(N_GROUPS)]
    picks, wts = [], []
    for k in range(TOP_K):
        m = sel_g[0]
        for g in range(1, N_GROUPS):
            m = jnp.maximum(m, sel_g[g])
        m = jnp.max(m, axis=0, keepdims=True)
        cand = jnp.where(sel_g[0] == m, eidx[0], N_EXPERTS)
        for g in range(1, N_GROUPS):
            cand = jnp.minimum(cand, jnp.where(sel_g[g] == m, eidx[g], N_EXPERTS))
        pick = jnp.min(cand, axis=0, keepdims=True)
        wk = jnp.zeros((per, tm), F32)
        for g in range(N_GROUPS):
            hit = eidx[g] == pick
            wk = wk + jnp.where(hit, s_g[g], 0.0)
            onehot[g] = onehot[g] + jnp.where(hit, 1.0, 0.0)
            sel_g[g] = jnp.where(hit, neg, sel_g[g])
        picks.append(pick)
        wts.append(jnp.sum(wk, axis=0, keepdims=True))
    wsum = wts[0]
    for k in range(1, TOP_K):
        wsum = wsum + wts[k]
    oh = jnp.concatenate(onehot, axis=0)
    carry = carry_ref[...][:, 0:1]
    before = _dot(oh.astype(BF16), upper_ref[...]) + carry
    for k in range(TOP_K):
        idx_ref[k:k + 1, :] = picks[k]
        w_ref[k:k + 1, :] = wts[k] / wsum * ROUTED_SCALE
        rk = jnp.zeros((per, tm), F32)
        for g in range(N_GROUPS):
            rk = rk + jnp.where(eidx[g] == picks[k], before[g * per:(g + 1) * per, :], 0.0)
        rank_ref[k:k + 1, :] = jnp.sum(rk, axis=0, keepdims=True).astype(I32)
    carry_ref[...] = carry_ref[...] + jnp.sum(oh, axis=1, keepdims=True)
    cnt_ref[...] = carry_ref[...]


def router(h2, router_wt, router_b):
    N, D = h2.shape
    tm = 512 if N % 512 == 0 else 256
    E = N_EXPERTS
    upper = (jnp.arange(tm)[:, None] < jnp.arange(tm)[None, :]).astype(BF16)
    kern = functools.partial(_router_kernel, tm=tm)
    return pl.pallas_call(
        kern,
        grid=(N // tm,),
        in_specs=[
            pl.BlockSpec((tm, D), lambda i: (i, 0)),
            pl.BlockSpec((E, D), lambda i: (0, 0)),
            pl.BlockSpec((E, LANES), lambda i: (0, 0)),
            pl.BlockSpec((tm, tm), lambda i: (0, 0)),
        ],
        out_specs=[
            pl.BlockSpec((TOP_K, tm), lambda i: (0, i)),
            pl.BlockSpec((TOP_K, tm), lambda i: (0, i)),
            pl.BlockSpec((TOP_K, tm), lambda i: (0, i)),
            pl.BlockSpec((E, LANES), lambda i: (0, 0)),
        ],
        out_shape=[jax.ShapeDtypeStruct((TOP_K, N), I32), jax.ShapeDtypeStruct((TOP_K, N), F32),
                   jax.ShapeDtypeStruct((TOP_K, N), I32), jax.ShapeDtypeStruct((E, LANES), F32)],
        scratch_shapes=[pltpu.VMEM((E, LANES), F32)],
        compiler_params=_cparams(("arbitrary",), 32),
        name="router",
    )(h2, router_wt, router_b, upper)


MOE_TM = 256


def _dispatch_kernel(fill_ref, dest_ref, src_ref, xs_ref, zero_ref, sem, zsem, *, tb, n_fill):
    @pl.when(pl.program_id(0) == 0)
    def _():
        zero_ref[...] = jnp.zeros_like(zero_ref)

        def zissue(r, c):
            pltpu.make_async_copy(zero_ref.at[pl.ds(0, 1)], xs_ref.at[pl.ds(fill_ref[r], 1)], zsem).start()
            return c

        lax.fori_loop(0, n_fill, zissue, 0, unroll=8)

        def zdrain(r, c):
            pltpu.make_async_copy(zero_ref.at[pl.ds(0, 1)], xs_ref.at[pl.ds(0, 1)], zsem).wait()
            return c

        lax.fori_loop(0, n_fill, zdrain, 0, unroll=8)

    def issue(r, c):
        for k in range(TOP_K):
            d = dest_ref[0, k * tb + r]
            pltpu.make_async_copy(src_ref.at[pl.ds(r, 1)], xs_ref.at[pl.ds(d, 1)], sem).start()
        return c

    lax.fori_loop(0, tb, issue, 0, unroll=2)

    def drain(r, c):
        for k in range(TOP_K):
            pltpu.make_async_copy(src_ref.at[pl.ds(0, 1)], xs_ref.at[pl.ds(0, 1)], sem).wait()
        return c

    lax.fori_loop(0, tb, drain, 0, unroll=2)


def _block_major(dest, tb):
    K, N = dest.shape
    return dest.reshape(K, N // tb, tb).transpose(1, 0, 2).reshape(N // tb, 1, K * tb)


def dispatch(src, dest, fill):
    N, W = src.shape
    n_fill = fill.shape[0]
    n_slots = TOP_K * N + n_fill
    tb = 256
    kern = functools.partial(_dispatch_kernel, tb=tb, n_fill=n_fill)
    return pl.pallas_call(
        kern,
        grid_spec=pltpu.PrefetchScalarGridSpec(
            num_scalar_prefetch=1,
            grid=(N // tb,),
            in_specs=[
                pl.BlockSpec((None, 1, TOP_K * tb), lambda i, fl: (i, 0, 0), memory_space=pltpu.SMEM),
                pl.BlockSpec((tb, W), lambda i, fl: (i, 0)),
            ],
            out_specs=pl.BlockSpec(memory_space=pl.ANY),
            scratch_shapes=[pltpu.VMEM((SUBLANES, W), src.dtype), pltpu.SemaphoreType.DMA(()),
                            pltpu.SemaphoreType.DMA(())],
        ),
        out_shape=jax.ShapeDtypeStruct((n_slots, W), src.dtype),
        compiler_params=_cparams(("arbitrary",), 16),
        name="dispatch",
    )(fill, _block_major(dest, tb), src)


def _expert_up_kernel(be_ref, nu_ref, nxt_ref, xs_ref, wg_ref, wu_ref, h_ref, stage_ref, wgb_ref, wub_ref, slot_ref,
                      sem, *, layer, tn):
    j = pl.program_id(0)
    i = pl.program_id(1)
    nu = nu_ref[0]
    half = xs_ref.shape[1]
    e = be_ref[i]
    fresh = (i == 0) | (e != be_ref[jnp.maximum(i - 1, 0)])
    c0 = pl.multiple_of(j * tn, tn)

    def copies(expert, slot):
        return [pltpu.make_async_copy(w.at[layer, expert, :, pl.ds(c0, tn)], stage_ref.at[slot, m], sem.at[slot, m])
                for m, w in enumerate((wg_ref, wu_ref))]

    @pl.when(i == 0)
    def _():
        slot_ref[0] = 0
        for cp in copies(e, 0):
            cp.start()

    @pl.when((i < nu) & fresh)
    def _():
        slot = slot_ref[0]
        for cp in copies(e, slot):
            cp.wait()
        rows = stage_ref.shape[2] // 16

        def cast(r, c):
            r0 = pl.multiple_of(r * rows, rows)
            wgb_ref[pl.ds(r0, rows), :] = stage_ref[slot, 0, pl.ds(r0, rows), :].astype(BF16)
            wub_ref[pl.ds(r0, rows), :] = stage_ref[slot, 1, pl.ds(r0, rows), :].astype(BF16)
            return c

        lax.fori_loop(0, 16, cast, 0)
        nxt = nxt_ref[i]

        @pl.when(nxt >= 0)
        def _():
            for cp in copies(nxt, 1 - slot):
                cp.start()

        slot_ref[0] = 1 - slot

    @pl.when(i < nu)
    def _():
        u = xs_ref[...]
        lo = pltpu.bitcast(u << 16, F32).astype(BF16)
        hi = pltpu.bitcast(u & jnp.uint32(0xFFFF0000), F32).astype(BF16)
        g = _dot(lo, wgb_ref[0:half, :]) + _dot(hi, wgb_ref[half:2 * half, :])
        up = _dot(lo, wub_ref[0:half, :]) + _dot(hi, wub_ref[half:2 * half, :])
        h_ref[...] = (_silu(g) * up).astype(h_ref.dtype)

    @pl.when(i >= nu)
    def _():
        h_ref[...] = jnp.zeros_like(h_ref)


def expert_up(xs, blk_e, n_used, nxt_e, w_gate, w_up, layer):
    S, half = xs.shape
    _, E, D, Fd = w_gate.shape
    tm, tn = MOE_TM, 512
    nb = S // tm
    kern = functools.partial(_expert_up_kernel, layer=layer, tn=tn)
    return pl.pallas_call(
        kern,
        grid_spec=pltpu.PrefetchScalarGridSpec(
            num_scalar_prefetch=3,
            grid=(Fd // tn, nb),
            in_specs=[
                pl.BlockSpec((tm, half), lambda j, i, be, nu, nx: (jnp.minimum(i, nu[0] - 1), 0)),
                pl.BlockSpec(memory_space=pl.ANY),
                pl.BlockSpec(memory_space=pl.ANY),
            ],
            out_specs=pl.BlockSpec((tm, tn), lambda j, i, be, nu, nx: (i, j)),
            scratch_shapes=[pltpu.VMEM((2, 2, D, tn), F32), pltpu.VMEM((D, tn), BF16), pltpu.VMEM((D, tn), BF16),
                            pltpu.SMEM((1,), I32), pltpu.SemaphoreType.DMA((2, 2))],
        ),
        out_shape=jax.ShapeDtypeStruct((S, Fd), BF16),
        compiler_params=_cparams(("arbitrary", "arbitrary"), 62),
        name="expert_up",
    )(blk_e, n_used, nxt_e, xs, w_gate, w_up)


def _expert_down_kernel(be_ref, nu_ref, nxt_ref, h_ref, wd_ref, y_ref, stage_ref, wdb_ref, slot_ref, sem, *, layer):
    i = pl.program_id(0)
    nu = nu_ref[0]
    e = be_ref[i]
    fresh = (i == 0) | (e != be_ref[jnp.maximum(i - 1, 0)])

    def copy(expert, slot):
        return pltpu.make_async_copy(wd_ref.at[layer, expert], stage_ref.at[slot], sem.at[slot])

    @pl.when(i == 0)
    def _():
        slot_ref[0] = 0
        copy(e, 0).start()

    @pl.when((i < nu) & fresh)
    def _():
        slot = slot_ref[0]
        copy(e, slot).wait()
        rows = stage_ref.shape[1] // 16

        def cast(r, c):
            r0 = pl.multiple_of(r * rows, rows)
            wdb_ref[pl.ds(r0, rows), :] = stage_ref[slot, pl.ds(r0, rows), :].astype(BF16)
            return c

        lax.fori_loop(0, 16, cast, 0)
        nxt = nxt_ref[i]

        @pl.when(nxt >= 0)
        def _():
            copy(nxt, 1 - slot).start()

        slot_ref[0] = 1 - slot

    @pl.when(i < nu)
    def _():
        y_ref[...] = _dot(h_ref[...], wdb_ref[...])

    @pl.when(i >= nu)
    def _():
        y_ref[...] = jnp.zeros_like(y_ref)


def expert_down(h, blk_e, n_used, nxt_e, w_down, layer):
    S, Fd = h.shape
    _, E, _, D = w_down.shape
    tm = MOE_TM
    nb = S // tm
    kern = functools.partial(_expert_down_kernel, layer=layer)
    return pl.pallas_call(
        kern,
        grid_spec=pltpu.PrefetchScalarGridSpec(
            num_scalar_prefetch=3,
            grid=(nb,),
            in_specs=[
                pl.BlockSpec((tm, Fd), lambda i, be, nu, nx: (jnp.minimum(i, nu[0] - 1), 0)),
                pl.BlockSpec(memory_space=pl.ANY),
            ],
            out_specs=pl.BlockSpec((tm, D), lambda i, be, nu, nx: (i, 0)),
            scratch_shapes=[pltpu.VMEM((2, Fd, D), F32), pltpu.VMEM((Fd, D), BF16), pltpu.SMEM((1,), I32),
                            pltpu.SemaphoreType.DMA((2,))],
        ),
        out_shape=jax.ShapeDtypeStruct((S, D), F32),
        compiler_params=_cparams(("arbitrary",), 56),
        name="expert_down",
    )(blk_e, n_used, nxt_e, h, w_down)


def _combine_kernel(dcur_ref, dnext_ref, y_ref, w_ref, sh_ref, x1_ref, g_ref, mod_ref, o_ref, buf_ref, sem, *, tb, nb):
    i = pl.program_id(0)
    cur = lax.rem(i, 2)

    def fetch(dref, half):
        def issue(r, c):
            for k in range(TOP_K):
                d = dref[0, k * tb + r]
                pltpu.make_async_copy(y_ref.at[pl.ds(d, 1)], buf_ref.at[half, k, pl.ds(r, 1)], sem.at[half]).start()
            return c

        lax.fori_loop(0, tb, issue, 0, unroll=2)

    @pl.when(i == 0)
    def _():
        fetch(dcur_ref, 0)

    @pl.when(i + 1 < nb)
    def _():
        fetch(dnext_ref, 1 - cur)

    def drain(r, c):
        for k in range(TOP_K):
            pltpu.make_async_copy(y_ref.at[pl.ds(0, 1)], buf_ref.at[cur, k, pl.ds(0, 1)], sem.at[cur]).wait()
        return c

    lax.fori_loop(0, tb, drain, 0, unroll=2)
    w = w_ref[...]
    acc = sh_ref[...]
    for k in range(TOP_K):
        acc = acc + w[:, k:k + 1] * buf_ref[cur, k]
    o_ref[...] = x1_ref[...] + mod_ref[5:6, :] * (_rms(acc) * g_ref[3:4, :])


def combine(y_sorted, dest, w, shared, x1, g, mods, n_ctx, t_lat):
    N, D = x1.shape
    tb = 64
    nb = N // tb
    dest3 = _block_major(dest, tb)
    kern = functools.partial(_combine_kernel, tb=tb, nb=nb)
    return pl.pallas_call(
        kern,
        grid=(nb,),
        in_specs=[
            pl.BlockSpec((None, 1, TOP_K * tb), lambda i: (i, 0, 0), memory_space=pltpu.SMEM),
            pl.BlockSpec((None, 1, TOP_K * tb), lambda i: (jnp.minimum(i + 1, nb - 1), 0, 0), memory_space=pltpu.SMEM),
            pl.BlockSpec(memory_space=pl.ANY),
            pl.BlockSpec((tb, TOP_K), lambda i: (i, 0)),
            pl.BlockSpec((tb, D), lambda i: (i, 0)),
            pl.BlockSpec((tb, D), lambda i: (i, 0)),
            pl.BlockSpec((4, D), lambda i: (0, 0)),
            pl.BlockSpec((None, 6, D), lambda i: (_mod_row_index(i, tb, n_ctx, t_lat), 0, 0)),
        ],
        out_specs=pl.BlockSpec((tb, D), lambda i: (i, 0)),
        out_shape=jax.ShapeDtypeStruct((N, D), F32),
        scratch_shapes=[pltpu.VMEM((2, TOP_K, tb, D), F32), pltpu.SemaphoreType.DMA((2,))],
        compiler_params=_cparams(("arbitrary",), 40),
        name="combine",
    )(dest3, dest3, y_sorted, w, shared, x1, g, mods)


def moe_ffn(h2, hp, x1, p, g, mods, n_ctx, t_lat, layer):
    N, D = x1.shape
    tm = MOE_TM
    idx, w, rank, cnt = router(h2, p['router_wt'], p['router_b'])
    counts = cnt[:, 0].astype(I32)
    padded = (counts + tm - 1) // tm * tm
    pend = jnp.cumsum(padded)
    pstart = pend - padded
    n_blocks = (N * TOP_K + N_EXPERTS * (tm - 1) + tm - 1) // tm
    n_used = (pend[-1] // tm).astype(I32).reshape(1)
    blk_start = jnp.arange(n_blocks, dtype=I32) * tm
    blk_e = jnp.minimum(jnp.sum((pend[None, :] <= blk_start[:, None]).astype(I32), axis=1), N_EXPERTS - 1)
    hit = idx[:, :, None] == jnp.arange(N_EXPERTS, dtype=I32)[None, None, :]
    dest = jnp.sum(jnp.where(hit, pstart[None, None, :], 0), axis=-1) + rank
    after = jnp.sum(jnp.where(blk_e[:, None] == jnp.arange(N_EXPERTS, dtype=I32)[None, :], pend[None, :], 0),
                    axis=1) // tm
    nxt_e = jnp.where(after < n_used[0], blk_e[jnp.minimum(after, n_blocks - 1)], -1).astype(I32)
    n_fill = n_blocks * tm - N * TOP_K
    pad = padded - counts
    pad_end = jnp.cumsum(pad)
    q = jnp.arange(n_fill, dtype=I32)
    owner = jnp.sum((pad_end[None, :] <= q[:, None]).astype(I32), axis=1)
    oh = owner[:, None] == jnp.arange(N_EXPERTS, dtype=I32)[None, :]
    base = jnp.sum(jnp.where(oh, (pstart + counts - (pad_end - pad))[None, :], 0), axis=1)
    fill = jnp.where(owner < N_EXPERTS, base + q, pend[-1] + (q - pad_end[-1])).astype(I32)
    xs = dispatch(hp, dest, fill)
    hid = expert_up(xs, blk_e, n_used, nxt_e, p['exp_w_gate'], p['exp_w_up'], layer)
    ys = expert_down(hid, blk_e, n_used, nxt_e, p['exp_w_down'], layer)
    sh_hid = swiglu(h2, p['sh_w_gate'], p['sh_w_up'])
    shared = matmul(sh_hid, p['sh_w_down'])
    return combine(ys, dest, w.T, shared, x1, g, mods, n_ctx, t_lat)


def _axial_rope_tables(n):
    rows = n // GRID_W
    t = jnp.arange(rows * GRID_W)
    row = (t // GRID_W).astype(F32)
    col = (t % GRID_W).astype(F32)
    n_freq = LANES // 4
    inv = ROPE_BASE ** (-jnp.arange(n_freq, dtype=F32) / n_freq)
    ang = jnp.concatenate([row[:, None] * inv, col[:, None] * inv], axis=-1)
    cos, sin = jnp.cos(ang), jnp.sin(ang)
    return jnp.concatenate([cos, cos], axis=-1), jnp.concatenate([-sin, sin], axis=-1)


def _layer_params(l, w_in, conv_w, conv_b, lru_wa, lru_ba, lru_wx, lru_bx, lru_lam, diff_lam, diff_subln,
                  gla_w, gla_b, gla_norm, w_out, router_w, router_b, exp_w_gate, exp_w_up, exp_w_down,
                  sh_w_gate, sh_w_up, sh_w_down):
    d_lru = conv_w.shape[-1]
    n_main = w_in.shape[-1] - 2 * GLA_RANK
    dk = LANES
    wa, wx = lru_wa[l], lru_wx[l]
    w_gate = jnp.concatenate([wa[0], wx[0], wa[1], wx[1]], axis=-1).astype(BF16)
    ba = lru_ba[l].reshape(2, H_LRU, 1, -1)
    bx = lru_bx[l].reshape(2, H_LRU, 1, -1)
    b_gate = jnp.concatenate([ba[0], bx[0], ba[1], bx[1]], axis=-1)
    gw = gla_w[l]
    w_glog = jnp.zeros((H_GLA, LANES, 2 * dk), F32)
    for d in range(2):
        blk = gw[d].reshape(GLA_RANK, H_GLA, dk).transpose(1, 0, 2)
        w_glog = w_glog.at[:, d * GLA_RANK:(d + 1) * GLA_RANK, d * dk:(d + 1) * dk].set(blk)
    gb = gla_b[l].reshape(2, H_GLA, 1, dk)
    b_glog = jnp.concatenate([gb[0], gb[1]], axis=-1)
    w_lr = jnp.zeros((w_in.shape[1], LANES), F32).at[:, :2 * GLA_RANK].set(w_in[l][:, n_main:])
    return {
        'w_main': w_in[l][:, :n_main].astype(BF16), 'w_lr': w_lr.astype(BF16),
        'conv_w': conv_w[l], 'conv_b': conv_b[l].reshape(1, d_lru),
        'w_gate': w_gate, 'b_gate': b_gate, 'lru_lam': lru_lam[l],
        'diff_lam': diff_lam[l], 'diff_subln': diff_subln[l].reshape(1, -1),
        'w_glog': w_glog.astype(BF16), 'b_glog': b_glog, 'gla_norm': gla_norm[l].reshape(1, -1),
        'w_out': w_out[l].astype(BF16),
        'router_wt': router_w[l].T.astype(BF16),
        'router_b': jnp.broadcast_to(router_b[l][:, None], (N_EXPERTS, LANES)),
        'exp_w_gate': exp_w_gate, 'exp_w_up': exp_w_up, 'exp_w_down': exp_w_down,
        'sh_w_gate': sh_w_gate[l].astype(BF16), 'sh_w_up': sh_w_up[l].astype(BF16),
        'sh_w_down': sh_w_down[l].astype(BF16),
    }


def kernel(x_prompt, x_sample, cache_k, cache_v, state_lru, state_gla, c, c_ctx, mod_w, mod_b, norm_g, w_in,
           conv_w, conv_b, lru_wa, lru_ba, lru_wx, lru_bx, lru_lam, diff_lam, diff_subln, gla_w, gla_b,
           gla_norm, w_out, router_w, router_b, exp_w_gate, exp_w_up, exp_w_down, sh_w_gate, sh_w_up,
           sh_w_down):
    Bc, Tc, D = x_prompt.shape
    Bl, Tl, _ = x_sample.shape
    L = mod_w.shape[0]
    n_ctx, n_lat = Bc * Tc, Bl * Tl
    d_lru = conv_w.shape[-1]
    d_diff = H_DIFF * 2 * LANES
    d_gk = H_GLA * LANES
    d_gv = H_GLA * 2 * LANES
    col_qb = 2 * d_lru
    col_qc = col_qb + 3 * d_diff
    col_kc = col_qc + d_gk
    col_vc = col_kc + d_gk
    col_gc = col_vc + d_gv

    x = jnp.concatenate([x_prompt.reshape(n_ctx, D), x_sample.reshape(n_lat, D)], axis=0)
    n_rows = 16 * ((1 + Bl + 15) // 16)
    c_rows = jnp.zeros((n_rows, D), F32).at[0].set(c_ctx).at[1:1 + Bl].set(c)
    mods_all = modulation(c_rows, mod_w, mod_b).reshape(L, n_rows, 6, D)
    rope = _axial_rope_tables(Tl)
    cache_k2 = cache_k.reshape(*cache_k.shape[:3], -1)
    cache_v2 = cache_v.reshape(*cache_v.shape[:3], -1)
    zeros_lru = jnp.zeros((Bc, 2, d_lru), F32)
    zeros_gla = jnp.zeros((Bc, 2, H_GLA, LANES, 2 * LANES), F32)

    ks, vs, lrus, glas = [], [], [], []
    for l in range(L):
        p = _layer_params(l, w_in, conv_w, conv_b, lru_wa, lru_ba, lru_wx, lru_bx, lru_lam, diff_lam,
                          diff_subln, gla_w, gla_b, gla_norm, w_out, router_w, router_b, exp_w_gate,
                          exp_w_up, exp_w_down, sh_w_gate, sh_w_up, sh_w_down)
        lam_init = 0.8 - 0.6 * math.exp(-0.3 * l)
        mods = mods_all[l]
        g = norm_g[l]
        h = norm_mod(x, g[0:1], mods, n_ctx, Tl)
        z = matmul(h, p['w_main'])
        z_lr = matmul(h, p['w_lr'])
        ycat = jnp.zeros((n_ctx + n_lat, D), BF16)
        ycat, lru_c = lru_mixer(z, ycat, 0, Bc, Tc, p['conv_w'], p['conv_b'], p['w_gate'], p['b_gate'],
                                p['lru_lam'], zeros_lru)
        ycat, _ = lru_mixer(z, ycat, n_ctx, Bl, Tl, p['conv_w'], p['conv_b'], p['w_gate'], p['b_gate'],
                            p['lru_lam'], state_lru[:, l])
        ycat = diff_attention(z, ycat, d_lru, 0, Bc, Tc, col_qb, None, None, l, None, p['diff_lam'],
                              p['diff_subln'], lam_init)
        ycat = diff_attention(z, ycat, d_lru, n_ctx, Bl, Tl, col_qb, cache_k2, cache_v2, l, rope, p['diff_lam'],
                              p['diff_subln'], lam_init)
        ycat, gla_c = gla_mixer(z, z_lr, ycat, d_lru + d_diff, 0, Bc, Tc, col_qc, col_kc, col_vc, col_gc,
                                p['w_glog'], p['b_glog'], p['gla_norm'], zeros_gla)
        ycat, _ = gla_mixer(z, z_lr, ycat, d_lru + d_diff, n_ctx, Bl, Tl, col_qc, col_kc, col_vc, col_gc,
                            p['w_glog'], p['b_glog'], p['gla_norm'], state_gla[:, l])
        y = matmul(ycat, p['w_out'])
        x1, h2, hp = post_mix(x, y, g, mods, n_ctx, Tl)
        x = moe_ffn(h2, hp, x1, p, g, mods, n_ctx, Tl, l)
        ks.append(z[:n_ctx, col_qb + d_diff:col_qb + 2 * d_diff].reshape(Bc, Tc, H_DIFF, 2 * LANES))
        vs.append(z[:n_ctx, col_qb + 2 * d_diff:col_qb + 3 * d_diff].reshape(Bc, Tc, H_DIFF, 2 * LANES))
        lrus.append(lru_c)
        glas.append(gla_c)
    yp = x[:n_ctx].reshape(Bc, Tc, D)
    ys = x[n_ctx:].reshape(Bl, Tl, D)
    return (yp, ys, jnp.stack(ks, axis=1), jnp.stack(vs, axis=1), jnp.stack(lrus, axis=1), jnp.stack(glas, axis=1))
```

```python
import functools
import math

import jax
import jax.numpy as jnp
from jax import lax
from jax.experimental import pallas as pl
from jax.experimental.pallas import tpu as pltpu

F32 = jnp.float32
BF16 = jnp.bfloat16
U32 = jnp.uint32
I32 = jnp.int32

EPS = 1e-6
GRID_W = 64
H_LRU = 8
CONV_W = 4
CONV_LEFT = 2
LRU_C = 8.0
H_DIFF = 8
ROPE_BASE = 10000.0
H_GLA = 4
GLA_RANK = 16
GLA_TAU = 16.0
GLA_CHUNK = 64
N_EXPERTS = 64
TOP_K = 8
N_GROUPS = 8
TOPK_GROUPS = 4
ROUTED_SCALE = 2.5

V7X_VMEM_BYTES = 64 * 1024 * 1024
LANES = 128
SUBLANES = 8
MIB = 1024 * 1024


def _cparams(semantics, vmem_mib):
    assert vmem_mib * MIB < V7X_VMEM_BYTES
    return pltpu.CompilerParams(dimension_semantics=semantics, vmem_limit_bytes=vmem_mib * MIB)


def _rms(x):
    return x * lax.rsqrt(jnp.mean(x * x, axis=-1, keepdims=True) + EPS)


def _sigmoid(x):
    return 1.0 / (1.0 + jnp.exp(-x))


def _silu(x):
    return x * _sigmoid(x)


def _dot(a, b):
    return jnp.dot(a, b, preferred_element_type=F32)


def _dot_nt(a, b):
    return lax.dot_general(a, b, (((1,), (1,)), ((), ())), preferred_element_type=F32)


def _dot_tn(a, b):
    return lax.dot_general(a, b, (((0,), (0,)), ((), ())), preferred_element_type=F32)


def _mod_kernel(c_ref, w_ref, b_ref, o_ref):
    s = _silu(c_ref[...]).astype(BF16)
    o_ref[...] = _dot(s, w_ref[...].astype(BF16)) + b_ref[...]


def modulation(c_rows, mod_w, mod_b):
    L, D, N6 = mod_w.shape
    R = c_rows.shape[0]
    tn = 512
    return pl.pallas_call(
        _mod_kernel,
        grid=(L, N6 // tn),
        in_specs=[
            pl.BlockSpec((R, D), lambda l, j: (0, 0)),
            pl.BlockSpec((None, D, tn), lambda l, j: (l, 0, j)),
            pl.BlockSpec((None, 1, tn), lambda l, j: (l, 0, j)),
        ],
        out_specs=pl.BlockSpec((None, R, tn), lambda l, j: (l, 0, j)),
        out_shape=jax.ShapeDtypeStruct((L, R, N6), F32),
        compiler_params=_cparams(("arbitrary", "arbitrary"), 40),
        name="modulation",
    )(c_rows, mod_w, mod_b.reshape(L, 1, N6))


def _mod_row_index(i, tm, n_ctx, t_lat):
    r0 = i * tm
    return jnp.where(r0 < n_ctx, 0, 1 + (r0 - n_ctx) // t_lat)


def _norm_mod_kernel(x_ref, g_ref, mod_ref, o_ref):
    y = _rms(x_ref[...]) * g_ref[...]
    o_ref[...] = (y * (1.0 + mod_ref[1:2, :]) + mod_ref[0:1, :]).astype(o_ref.dtype)


def norm_mod(x, g, mods, n_ctx, t_lat):
    N, D = x.shape
    tm = 256
    return pl.pallas_call(
        _norm_mod_kernel,
        grid=(N // tm,),
        in_specs=[
            pl.BlockSpec((tm, D), lambda i: (i, 0)),
            pl.BlockSpec((1, D), lambda i: (0, 0)),
            pl.BlockSpec((None, 6, D), lambda i: (_mod_row_index(i, tm, n_ctx, t_lat), 0, 0)),
        ],
        out_specs=pl.BlockSpec((tm, D), lambda i: (i, 0)),
        out_shape=jax.ShapeDtypeStruct((N, D), BF16),
        compiler_params=_cparams(("arbitrary",), 32),
        name="norm_mod",
    )(x, g, mods)


def _mm_kernel(x_ref, w_ref, o_ref):
    o_ref[...] = _dot(x_ref[...], w_ref[...]).astype(o_ref.dtype)


def matmul(x, w, out_dtype=F32, tm=1024, tn=512):
    M, K = x.shape
    N = w.shape[1]
    tm = min(tm, M)
    tn = min(tn, N)
    assert M % tm == 0 and N % tn == 0
    return pl.pallas_call(
        _mm_kernel,
        grid=(M // tm, N // tn),
        in_specs=[pl.BlockSpec((tm, K), lambda i, j: (i, 0)), pl.BlockSpec((K, tn), lambda i, j: (0, j))],
        out_specs=pl.BlockSpec((tm, tn), lambda i, j: (i, j)),
        out_shape=jax.ShapeDtypeStruct((M, N), out_dtype),
        compiler_params=_cparams(("arbitrary", "arbitrary"), 48),
        name="matmul",
    )(x, w)


def _swiglu_kernel(x_ref, wg_ref, wu_ref, o_ref):
    x = x_ref[...]
    o_ref[...] = (_silu(_dot(x, wg_ref[...])) * _dot(x, wu_ref[...])).astype(o_ref.dtype)


def swiglu(x, wg, wu, tm=1024, tn=256):
    M, K = x.shape
    N = wg.shape[1]
    tm = min(tm, M)
    return pl.pallas_call(
        _swiglu_kernel,
        grid=(M // tm, N // tn),
        in_specs=[
            pl.BlockSpec((tm, K), lambda i, j: (i, 0)),
            pl.BlockSpec((K, tn), lambda i, j: (0, j)),
            pl.BlockSpec((K, tn), lambda i, j: (0, j)),
        ],
        out_specs=pl.BlockSpec((tm, tn), lambda i, j: (i, j)),
        out_shape=jax.ShapeDtypeStruct((M, N), BF16),
        compiler_params=_cparams(("arbitrary", "arbitrary"), 48),
        name="swiglu",
    )(x, wg, wu)


def _lru_kernel(xa_ref, ga_ref, cw_ref, cb_ref, wg_ref, bg_ref, lam_ref, h0_ref, ybuf_ref, y_ref, hl_ref,
                pad_ref, af_ref, uf_ref, ab_ref, ub_ref, *, T, CW):
    del ybuf_ref
    pad_ref[0:SUBLANES, :] = jnp.zeros((SUBLANES, CW), F32)
    pad_ref[SUBLANES + T:2 * SUBLANES + T, :] = jnp.zeros((SUBLANES, CW), F32)
    pad_ref[SUBLANES:SUBLANES + T, :] = xa_ref[...]
    xc = cb_ref[...] + jnp.zeros((T, CW), F32)
    for j in range(CONV_W):
        o = SUBLANES - CONV_LEFT + j
        xc = xc + cw_ref[j:j + 1, :] * pad_ref[o:o + T, :]
    lam = lam_ref[...]
    nl = -lam
    softplus = jnp.maximum(nl, 0.0) + jnp.log(1.0 + jnp.exp(-jnp.abs(nl)))
    bs = LANES
    for hh in range(CW // bs):
        cs = slice(hh * bs, (hh + 1) * bs)
        xb = xc[:, cs]
        gts = _dot(xb.astype(BF16), wg_ref[hh]) + bg_ref[hh]
        for d, (a_ref, u_ref) in enumerate(((af_ref, uf_ref), (ab_ref, ub_ref))):
            r = _sigmoid(gts[:, (2 * d) * bs:(2 * d + 1) * bs])
            ig = _sigmoid(gts[:, (2 * d + 1) * bs:(2 * d + 2) * bs])
            log_a = (-LRU_C) * r * softplus[d:d + 1, cs]
            a = jnp.exp(log_a)
            a_ref[:, cs] = a
            u_ref[:, cs] = jnp.sqrt(1.0 - a * a) * (ig * xb)

    def step(t, carry):
        hf, hb = carry
        hf = af_ref[pl.ds(t, 1), :] * hf + uf_ref[pl.ds(t, 1), :]
        uf_ref[pl.ds(t, 1), :] = hf
        tb = T - 1 - t
        hb = ab_ref[pl.ds(tb, 1), :] * hb + ub_ref[pl.ds(tb, 1), :]
        ub_ref[pl.ds(tb, 1), :] = hb
        return hf, hb

    hf, hb = lax.fori_loop(0, T, step, (h0_ref[0:1, :], h0_ref[1:2, :]), unroll=8)
    hl_ref[0:1, :] = hf
    hl_ref[1:2, :] = hb
    y_ref[...] = ((uf_ref[...] + ub_ref[...]) * jax.nn.gelu(ga_ref[...], approximate=True)).astype(y_ref.dtype)


def lru_mixer(z, ybuf, row0, B, T, conv_w, conv_b, w_gate, b_gate, lam, h0):
    d_lru = conv_w.shape[1]
    CW = 512
    nj = d_lru // CW
    rb0 = row0 // T
    kern = functools.partial(_lru_kernel, T=T, CW=CW)
    return pl.pallas_call(
        kern,
        grid=(B, nj),
        in_specs=[
            pl.BlockSpec((T, CW), lambda b, j: (rb0 + b, j)),
            pl.BlockSpec((T, CW), lambda b, j: (rb0 + b, nj + j)),
            pl.BlockSpec((CONV_W, CW), lambda b, j: (0, j)),
            pl.BlockSpec((1, CW), lambda b, j: (0, j)),
            pl.BlockSpec((CW // LANES, LANES, 4 * LANES), lambda b, j: (j, 0, 0)),
            pl.BlockSpec((CW // LANES, 1, 4 * LANES), lambda b, j: (j, 0, 0)),
            pl.BlockSpec((2, CW), lambda b, j: (0, j)),
            pl.BlockSpec((None, 2, CW), lambda b, j: (b, 0, j)),
            pl.BlockSpec(memory_space=pl.ANY),
        ],
        out_specs=[
            pl.BlockSpec((T, CW), lambda b, j: (rb0 + b, j)),
            pl.BlockSpec((None, 2, CW), lambda b, j: (b, 0, j)),
        ],
        out_shape=[jax.ShapeDtypeStruct(ybuf.shape, ybuf.dtype), jax.ShapeDtypeStruct((B, 2, d_lru), F32)],
        scratch_shapes=[pltpu.VMEM((T + 2 * SUBLANES, CW), F32)] + [pltpu.VMEM((T, CW), F32)] * 4,
        input_output_aliases={8: 0},
        compiler_params=_cparams(("arbitrary", "arbitrary"), 40),
        name="lru_mixer",
    )(z, z, conv_w, conv_b, w_gate, b_gate, lam, h0, ybuf)


def _rope(x, cos2, sin2):
    return x * cos2 + pltpu.roll(x, LANES // 2, axis=1) * sin2


def _attn_kernel(*refs, T, P, tq, use_rope, lam_init):
    it = iter(refs)
    q_ref, k_ref, v_ref = next(it), next(it), next(it)
    ck_ref = cv_ref = cos_ref = sin_ref = None
    if P:
        ck_ref, cv_ref = next(it), next(it)
    if use_rope:
        cos_ref, sin_ref = next(it), next(it)
    dl_ref, sub_ref, _, o_ref, kall_ref, vall_ref = next(it), next(it), next(it), next(it), next(it), next(it)
    dh = LANES
    qi = pl.program_id(2)

    @pl.when(qi == 0)
    def _():
        if P:
            kall_ref[0:P, :] = ck_ref[...].astype(BF16)
            vall_ref[0:P, :] = cv_ref[...].astype(BF16)
        for m in range(2):
            km = k_ref[:, m * dh:(m + 1) * dh]
            if use_rope:
                km = _rope(km, cos_ref[...], sin_ref[...])
            kall_ref[P:P + T, m * dh:(m + 1) * dh] = km.astype(BF16)
        vall_ref[P:P + T, :] = v_ref[...].astype(BF16)

    dl = dl_ref[...]
    lam = (jnp.exp(jnp.sum(dl[0:1, :] * dl[1:2, :], axis=-1, keepdims=True))
           - jnp.exp(jnp.sum(dl[2:3, :] * dl[3:4, :], axis=-1, keepdims=True)) + lam_init)
    qscale = (dh ** -0.5) * math.log2(math.e)
    pv, invs = [], []
    for m in range(2):
        qm = q_ref[:, m * dh:(m + 1) * dh]
        if use_rope:
            r0 = pl.multiple_of(qi * tq, tq)
            qm = _rope(qm, cos_ref[pl.ds(r0, tq), :], sin_ref[pl.ds(r0, tq), :])
        s2 = _dot_nt((qm * qscale).astype(BF16), kall_ref[:, m * dh:(m + 1) * dh])
        e = jnp.exp2(s2 - jnp.max(s2, axis=-1, keepdims=True))
        invs.append(1.0 / jnp.sum(e, axis=-1, keepdims=True))
        pv.append(_dot(e.astype(BF16), vall_ref[...]))
    o = pv[0] * invs[0] - pv[1] * (lam * invs[1])
    o_ref[...] = (_rms(o) * sub_ref[...] * (1.0 - lam_init)).astype(o_ref.dtype)


def diff_attention(z, ybuf, col_y, row0, B, T, col_q, cache_k, cache_v, layer, rope, diff_lam, subln, lam_init):
    H = H_DIFF
    dw = 2 * LANES
    P = 0 if cache_k is None else cache_k.shape[2]
    tq = min(T, 1024)
    nq = T // tq
    rbq = row0 // tq
    rbk = row0 // T
    cq = col_q // dw
    kern = functools.partial(_attn_kernel, T=T, P=P, tq=tq, use_rope=rope is not None, lam_init=lam_init)
    in_specs = [
        pl.BlockSpec((tq, dw), lambda b, h, i: (rbq + b * nq + i, cq + h)),
        pl.BlockSpec((T, dw), lambda b, h, i: (rbk + b, cq + H + h)),
        pl.BlockSpec((T, dw), lambda b, h, i: (rbk + b, cq + 2 * H + h)),
    ]
    args = [z, z, z]
    if P:
        in_specs += [pl.BlockSpec((None, None, P, dw), lambda b, h, i: (b, layer, 0, h))] * 2
        args += [cache_k, cache_v]
    if rope is not None:
        in_specs += [pl.BlockSpec((T, LANES), lambda b, h, i: (0, 0))] * 2
        args += [rope[0], rope[1]]
    in_specs += [pl.BlockSpec((4, LANES), lambda b, h, i: (0, 0)), pl.BlockSpec((1, dw), lambda b, h, i: (0, 0)),
                 pl.BlockSpec(memory_space=pl.ANY)]
    args += [diff_lam, subln, ybuf]
    cy = col_y // dw
    return pl.pallas_call(
        kern,
        grid=(B, H, nq),
        in_specs=in_specs,
        out_specs=pl.BlockSpec((tq, dw), lambda b, h, i: (rbq + b * nq + i, cy + h)),
        out_shape=jax.ShapeDtypeStruct(ybuf.shape, ybuf.dtype),
        scratch_shapes=[pltpu.VMEM((P + T, dw), BF16), pltpu.VMEM((P + T, dw), BF16)],
        input_output_aliases={len(args) - 1: 0},
        compiler_params=_cparams(("arbitrary", "arbitrary", "arbitrary"), 40),
        name="diff_attention",
    )(*args)


def _gla_kernel(q_ref, k_ref, v_ref, gc_ref, lr_ref, wgl_ref, bgl_ref, gn_ref, s0_ref, ybuf_ref, y_ref, so_ref,
                glog_ref, qt_ref, ec_ref, kv_ref, o_ref, s_ref, *, T):
    del ybuf_ref
    C = GLA_CHUNK
    dk = LANES
    nc = T // C
    gl = _dot(lr_ref[...].astype(BF16), wgl_ref[...]) + bgl_ref[...]
    glog_ref[...] = (jnp.minimum(gl, 0.0) - jnp.log(1.0 + jnp.exp(-jnp.abs(gl)))) * (1.0 / GLA_TAU)
    row = lax.broadcasted_iota(I32, (C, C), 0)
    col = lax.broadcasted_iota(I32, (C, C), 1)
    qscale = dk ** -0.5
    ones = jnp.ones((C, dk), BF16)

    def prepare(c, carry):
        r0 = pl.multiple_of(c * C, C)
        q = q_ref[pl.ds(r0, C), :] * qscale
        k = k_ref[pl.ds(r0, C), :]
        v = v_ref[pl.ds(r0, C), :].astype(BF16)
        for d in range(2):
            keep = (row >= col) if d == 0 else (row <= col)
            tri = jnp.where(keep, 1.0, 0.0).astype(BF16)
            g = glog_ref[pl.ds(r0, C), d * dk:(d + 1) * dk]
            g_hi = g.astype(BF16)
            g_lo = (g - g_hi.astype(F32)).astype(BF16)
            b = _dot(tri, g_hi) + _dot(tri, g_lo)
            b_end = b[C - 1:C, :] if d == 0 else b[0:1, :]
            qt = (q * jnp.exp(b)).astype(BF16)
            kt = (k * jnp.exp(-b)).astype(BF16)
            att = jnp.where(keep, _dot_nt(qt, kt), 0.0)
            qt_ref[d, pl.ds(r0, C), :] = qt
            o_ref[d, pl.ds(r0, C), :] = _dot(att.astype(BF16), v)
            kd = (k * jnp.exp(b_end - b)).astype(BF16)
            bcol = _dot_tn(g_hi, ones) + _dot_tn(g_lo, ones)
            ec_ref[d, c] = jnp.exp(bcol)
            kv_ref[d, c] = _dot_tn(kd, v)
        return carry

    lax.fori_loop(0, nc, prepare, 0, unroll=4)
    s_ref[...] = s0_ref[...]

    def advance(ci, carry):
        for d in range(2):
            c = ci if d == 0 else nc - 1 - ci
            r0 = pl.multiple_of(c * C, C)
            S = s_ref[d]
            o_ref[d, pl.ds(r0, C), :] = o_ref[d, pl.ds(r0, C), :] + _dot(qt_ref[d, pl.ds(r0, C), :], S.astype(BF16))
            ec = ec_ref[d, c]
            s_ref[d] = jnp.concatenate([ec, ec], axis=1) * S + kv_ref[d, c]
        return carry

    lax.fori_loop(0, nc, advance, 0)
    so_ref[...] = s_ref[...]
    oc = o_ref[0] + o_ref[1]
    y_ref[...] = (_rms(oc) * gn_ref[...] * _silu(gc_ref[...])).astype(y_ref.dtype)


def gla_mixer(z, col_lr, ybuf, col_y, row0, B, T, col_q, col_k, col_v, col_g, w_glog, b_glog, gnorm, s0):
    H = H_GLA
    dk, dv = LANES, 2 * LANES
    rb = row0 // T
    kern = functools.partial(_gla_kernel, T=T)
    return pl.pallas_call(
        kern,
        grid=(B, H),
        in_specs=[
            pl.BlockSpec((T, dk), lambda b, h: (rb + b, col_q // dk + h)),
            pl.BlockSpec((T, dk), lambda b, h: (rb + b, col_k // dk + h)),
            pl.BlockSpec((T, dv), lambda b, h: (rb + b, col_v // dv + h)),
            pl.BlockSpec((T, dv), lambda b, h: (rb + b, col_g // dv + h)),
            pl.BlockSpec((T, LANES), lambda b, h: (rb + b, col_lr // LANES)),
            pl.BlockSpec((None, LANES, 2 * dk), lambda b, h: (h, 0, 0)),
            pl.BlockSpec((None, 1, 2 * dk), lambda b, h: (h, 0, 0)),
            pl.BlockSpec((1, dv), lambda b, h: (0, 0)),
            pl.BlockSpec((None, 2, None, dk, dv), lambda b, h: (b, 0, h, 0, 0)),
            pl.BlockSpec(memory_space=pl.ANY),
        ],
        out_specs=[
            pl.BlockSpec((T, dv), lambda b, h: (rb + b, col_y // dv + h)),
            pl.BlockSpec((None, 2, None, dk, dv), lambda b, h: (b, 0, h, 0, 0)),
        ],
        out_shape=[jax.ShapeDtypeStruct(ybuf.shape, ybuf.dtype), jax.ShapeDtypeStruct((B, 2, H, dk, dv), F32)],
        scratch_shapes=[pltpu.VMEM((T, 2 * dk), F32), pltpu.VMEM((2, T, dk), BF16),
                        pltpu.VMEM((2, T // GLA_CHUNK, dk, dk), F32), pltpu.VMEM((2, T // GLA_CHUNK, dk, dv), F32),
                        pltpu.VMEM((2, T, dv), F32), pltpu.VMEM((2, dk, dv), F32)],
        input_output_aliases={9: 0},
        compiler_params=_cparams(("arbitrary", "arbitrary"), 32),
        name="gla_mixer",
    )(z, z, z, z, z, w_glog, b_glog, gnorm, s0, ybuf)


def _post_mix_kernel(x_ref, y_ref, g_ref, mod_ref, x1_ref, h2_ref, hp_ref):
    D = x_ref.shape[1]
    x1 = x_ref[...] + mod_ref[2:3, :] * (_rms(y_ref[...]) * g_ref[1:2, :])
    x1_ref[...] = x1
    h = (_rms(x1) * g_ref[2:3, :]) * (1.0 + mod_ref[4:5, :]) + mod_ref[3:4, :]
    hb = h.astype(BF16)
    h2_ref[...] = hb
    u = pltpu.bitcast(hb.astype(F32), U32)
    hp_ref[...] = (u[:, :D // 2] >> 16) | (u[:, D // 2:] & jnp.uint32(0xFFFF0000))


def post_mix(x, y, g, mods, n_ctx, t_lat):
    N, D = x.shape
    tm = 256
    return pl.pallas_call(
        _post_mix_kernel,
        grid=(N // tm,),
        in_specs=[
            pl.BlockSpec((tm, D), lambda i: (i, 0)),
            pl.BlockSpec((tm, D), lambda i: (i, 0)),
            pl.BlockSpec((4, D), lambda i: (0, 0)),
            pl.BlockSpec((None, 6, D), lambda i: (_mod_row_index(i, tm, n_ctx, t_lat), 0, 0)),
        ],
        out_specs=[
            pl.BlockSpec((tm, D), lambda i: (i, 0)),
            pl.BlockSpec((tm, D), lambda i: (i, 0)),
            pl.BlockSpec((tm, D // 2), lambda i: (i, 0)),
        ],
        out_shape=[jax.ShapeDtypeStruct((N, D), F32), jax.ShapeDtypeStruct((N, D), BF16),
                   jax.ShapeDtypeStruct((N, D // 2), U32)],
        compiler_params=_cparams(("arbitrary",), 48),
        name="post_mix",
    )(x, y, g, mods)


def _router_kernel(h_ref, rwt_ref, rb_ref, upper_ref, idx_ref, w_ref, rank_ref, cnt_ref, carry_ref, *, tm):
    i = pl.program_id(0)
    per = N_EXPERTS // N_GROUPS

    @pl.when(i == 0)
    def _():
        carry_ref[...] = jnp.zeros_like(carry_ref)

    logits = _dot_nt(rwt_ref[...], h_ref[...])
    s_all = _sigmoid(logits)
    sel_all = s_all + rb_ref[...][:, 0:1]
    sub = lax.broadcasted_iota(I32, (per, tm), 0)
    neg = -jnp.inf
    s_g, sel_g, gscore = [], [], []
    for g in range(N_GROUPS):
        blk = sel_all[g * per:(g + 1) * per, :]
        s_g.append(s_all[g * per:(g + 1) * per, :])
        sel_g.append(blk)
        m1 = jnp.max(blk, axis=0, keepdims=True)
        first = jnp.min(jnp.where(blk == m1, sub, per), axis=0, keepdims=True)
        m2 = jnp.max(jnp.where(sub == first, neg, blk), axis=0, keepdims=True)
        gscore.append(m1 + m2)
    for g in range(N_GROUPS):
        beat = jnp.zeros((1, tm), F32)
        for g2 in range(N_GROUPS):
            if g2 == g:
                continue
            wins = (gscore[g2] > gscore[g]) | ((gscore[g2] == gscore[g]) & (g2 < g))
            beat = beat + jnp.where(wins, 1.0, 0.0)
        sel_g[g] = jnp.where(beat < TOPK_GROUPS, sel_g[g], neg)
    eidx = [sub + g * per for g in range(N_GROUPS)]
    onehot = [jnp.zeros((per, tm), F32) for _ in range(N_GROUPS)]
    picks, wts = [], []
    for k in range(TOP_K):
        m = sel_g[0]
        for g in range(1, N_GROUPS):
            m = jnp.maximum(m, sel_g[g])
        m = jnp.max(m, axis=0, keepdims=True)
        cand = jnp.where(sel_g[0] == m, eidx[0], N_EXPERTS)
        for g in range(1, N_GROUPS):
            cand = jnp.minimum(cand, jnp.where(sel_g[g] == m, eidx[g], N_EXPERTS))
        pick = jnp.min(cand, axis=0, keepdims=True)
        wk = jnp.zeros((per, tm), F32)
        for g in range(N_GROUPS):
            hit = eidx[g] == pick
            wk = wk + jnp.where(hit, s_g[g], 0.0)
            onehot[g] = onehot[g] + jnp.where(hit, 1.0, 0.0)
            sel_g[g] = jnp.where(hit, neg, sel_g[g])
        picks.append(pick)
        wts.append(jnp.sum(wk, axis=0, keepdims=True))
    wsum = wts[0]
    for k in range(1, TOP_K):
        wsum = wsum + wts[k]
    oh = jnp.concatenate(onehot, axis=0)
    carry = carry_ref[...][:, 0:1]
    before = _dot(oh.astype(BF16), upper_ref[...]) + carry
    for k in range(TOP_K):
        idx_ref[k:k + 1, :] = picks[k]
        w_ref[k:k + 1, :] = wts[k] / wsum * ROUTED_SCALE
        rk = jnp.zeros((per, tm), F32)
        for g in range(N_GROUPS):
            rk = rk + jnp.where(eidx[g] == picks[k], before[g * per:(g + 1) * per, :], 0.0)
        rank_ref[k:k + 1, :] = jnp.sum(rk, axis=0, keepdims=True).astype(I32)
    carry_ref[...] = carry_ref[...] + jnp.sum(oh, axis=1, keepdims=True)
    cnt_ref[...] = carry_ref[...]


def router(h2, router_wt, router_b):
    N, D = h2.shape
    tm = 512 if N % 512 == 0 else 256
    E = N_EXPERTS
    upper = (jnp.arange(tm)[:, None] < jnp.arange(tm)[None, :]).astype(BF16)
    kern = functools.partial(_router_kernel, tm=tm)
    return pl.pallas_call(
        kern,
        grid=(N // tm,),
        in_specs=[
            pl.BlockSpec((tm, D), lambda i: (i, 0)),
            pl.BlockSpec((E, D), lambda i: (0, 0)),
            pl.BlockSpec((E, LANES), lambda i: (0, 0)),
            pl.BlockSpec((tm, tm), lambda i: (0, 0)),
        ],
        out_specs=[
            pl.BlockSpec((TOP_K, tm), lambda i: (0, i)),
            pl.BlockSpec((TOP_K, tm), lambda i: (0, i)),
            pl.BlockSpec((TOP_K, tm), lambda i: (0, i)),
            pl.BlockSpec((E, LANES), lambda i: (0, 0)),
        ],
        out_shape=[jax.ShapeDtypeStruct((TOP_K, N), I32), jax.ShapeDtypeStruct((TOP_K, N), F32),
                   jax.ShapeDtypeStruct((TOP_K, N), I32), jax.ShapeDtypeStruct((E, LANES), F32)],
        scratch_shapes=[pltpu.VMEM((E, LANES), F32)],
        compiler_params=_cparams(("arbitrary",), 32),
        name="router",
    )(h2, router_wt, router_b, upper)


MOE_TM = 256


def _dispatch_kernel(fill_ref, dest_ref, src_ref, xs_ref, zero_ref, sem, zsem, *, tb, n_fill):
    @pl.when(pl.program_id(0) == 0)
    def _():
        zero_ref[...] = jnp.zeros_like(zero_ref)

        def zissue(r, c):
            pltpu.make_async_copy(zero_ref.at[pl.ds(0, 1)], xs_ref.at[pl.ds(fill_ref[r], 1)], zsem).start()
            return c

        lax.fori_loop(0, n_fill, zissue, 0, unroll=8)

        def zdrain(r, c):
            pltpu.make_async_copy(zero_ref.at[pl.ds(0, 1)], xs_ref.at[pl.ds(0, 1)], zsem).wait()
            return c

        lax.fori_loop(0, n_fill, zdrain, 0, unroll=8)

    def issue(r, c):
        for k in range(TOP_K):
            d = dest_ref[0, k * tb + r]
            pltpu.make_async_copy(src_ref.at[pl.ds(r, 1)], xs_ref.at[pl.ds(d, 1)], sem).start()
        return c

    lax.fori_loop(0, tb, issue, 0, unroll=2)

    def drain(r, c):
        for k in range(TOP_K):
            pltpu.make_async_copy(src_ref.at[pl.ds(0, 1)], xs_ref.at[pl.ds(0, 1)], sem).wait()
        return c

    lax.fori_loop(0, tb, drain, 0, unroll=2)


def _block_major(dest, tb):
    K, N = dest.shape
    return dest.reshape(K, N // tb, tb).transpose(1, 0, 2).reshape(N // tb, 1, K * tb)


def dispatch(src, dest, fill):
    N, W = src.shape
    n_fill = fill.shape[0]
    n_slots = TOP_K * N + n_fill
    tb = 256
    kern = functools.partial(_dispatch_kernel, tb=tb, n_fill=n_fill)
    return pl.pallas_call(
        kern,
        grid_spec=pltpu.PrefetchScalarGridSpec(
            num_scalar_prefetch=1,
            grid=(N // tb,),
            in_specs=[
                pl.BlockSpec((None, 1, TOP_K * tb), lambda i, fl: (i, 0, 0), memory_space=pltpu.SMEM),
                pl.BlockSpec((tb, W), lambda i, fl: (i, 0)),
            ],
            out_specs=pl.BlockSpec(memory_space=pl.ANY),
            scratch_shapes=[pltpu.VMEM((SUBLANES, W), src.dtype), pltpu.SemaphoreType.DMA(()),
                            pltpu.SemaphoreType.DMA(())],
        ),
        out_shape=jax.ShapeDtypeStruct((n_slots, W), src.dtype),
        compiler_params=_cparams(("arbitrary",), 16),
        name="dispatch",
    )(fill, _block_major(dest, tb), src)


def _expert_up_kernel(be_ref, nu_ref, nxt_ref, xs_ref, wg_ref, wu_ref, h_ref, stage_ref, wgb_ref, wub_ref, slot_ref,
                      sem, *, layer, tn):
    j = pl.program_id(0)
    i = pl.program_id(1)
    nu = nu_ref[0]
    half = xs_ref.shape[1]
    e = be_ref[i]
    fresh = (i == 0) | (e != be_ref[jnp.maximum(i - 1, 0)])
    c0 = pl.multiple_of(j * tn, tn)

    def copies(expert, slot):
        return [pltpu.make_async_copy(w.at[layer, expert, :, pl.ds(c0, tn)], stage_ref.at[slot, m], sem.at[slot, m])
                for m, w in enumerate((wg_ref, wu_ref))]

    @pl.when(i == 0)
    def _():
        slot_ref[0] = 0
        for cp in copies(e, 0):
            cp.start()

    @pl.when((i < nu) & fresh)
    def _():
        slot = slot_ref[0]
        for cp in copies(e, slot):
            cp.wait()
        rows = stage_ref.shape[2] // 16

        def cast(r, c):
            r0 = pl.multiple_of(r * rows, rows)
            wgb_ref[pl.ds(r0, rows), :] = stage_ref[slot, 0, pl.ds(r0, rows), :].astype(BF16)
            wub_ref[pl.ds(r0, rows), :] = stage_ref[slot, 1, pl.ds(r0, rows), :].astype(BF16)
            return c

        lax.fori_loop(0, 16, cast, 0)
        nxt = nxt_ref[i]

        @pl.when(nxt >= 0)
        def _():
            for cp in copies(nxt, 1 - slot):
                cp.start()

        slot_ref[0] = 1 - slot

    @pl.when(i < nu)
    def _():
        u = xs_ref[...]
        lo = pltpu.bitcast(u << 16, F32).astype(BF16)
        hi = pltpu.bitcast(u & jnp.uint32(0xFFFF0000), F32).astype(BF16)
        g = _dot(lo, wgb_ref[0:half, :]) + _dot(hi, wgb_ref[half:2 * half, :])
        up = _dot(lo, wub_ref[0:half, :]) + _dot(hi, wub_ref[half:2 * half, :])
        h_ref[...] = (_silu(g) * up).astype(h_ref.dtype)

    @pl.when(i >= nu)
    def _():
        h_ref[...] = jnp.zeros_like(h_ref)


def expert_up(xs, blk_e, n_used, nxt_e, w_gate, w_up, layer):
    S, half = xs.shape
    _, E, D, Fd = w_gate.shape
    tm, tn = MOE_TM, 512
    nb = S // tm
    kern = functools.partial(_expert_up_kernel, layer=layer, tn=tn)
    return pl.pallas_call(
        kern,
        grid_spec=pltpu.PrefetchScalarGridSpec(
            num_scalar_prefetch=3,
            grid=(Fd // tn, nb),
            in_specs=[
                pl.BlockSpec((tm, half), lambda j, i, be, nu, nx: (jnp.minimum(i, nu[0] - 1), 0)),
                pl.BlockSpec(memory_space=pl.ANY),
                pl.BlockSpec(memory_space=pl.ANY),
            ],
            out_specs=pl.BlockSpec((tm, tn), lambda j, i, be, nu, nx: (i, j)),
            scratch_shapes=[pltpu.VMEM((2, 2, D, tn), F32), pltpu.VMEM((D, tn), BF16), pltpu.VMEM((D, tn), BF16),
                            pltpu.SMEM((1,), I32), pltpu.SemaphoreType.DMA((2, 2))],
        ),
        out_shape=jax.ShapeDtypeStruct((S, Fd), BF16),
        compiler_params=_cparams(("arbitrary", "arbitrary"), 62),
        name="expert_up",
    )(blk_e, n_used, nxt_e, xs, w_gate, w_up)


def _expert_down_kernel(be_ref, nu_ref, nxt_ref, h_ref, wd_ref, y_ref, stage_ref, wdb_ref, slot_ref, sem, *, layer):
    i = pl.program_id(0)
    nu = nu_ref[0]
    e = be_ref[i]
    fresh = (i == 0) | (e != be_ref[jnp.maximum(i - 1, 0)])

    def copy(expert, slot):
        return pltpu.make_async_copy(wd_ref.at[layer, expert], stage_ref.at[slot], sem.at[slot])

    @pl.when(i == 0)
    def _():
        slot_ref[0] = 0
        copy(e, 0).start()

    @pl.when((i < nu) & fresh)
    def _():
        slot = slot_ref[0]
        copy(e, slot).wait()
        rows = stage_ref.shape[1] // 16

        def cast(r, c):
            r0 = pl.multiple_of(r * rows, rows)
            wdb_ref[pl.ds(r0, rows), :] = stage_ref[slot, pl.ds(r0, rows), :].astype(BF16)
            return c

        lax.fori_loop(0, 16, cast, 0)
        nxt = nxt_ref[i]

        @pl.when(nxt >= 0)
        def _():
            copy(nxt, 1 - slot).start()

        slot_ref[0] = 1 - slot

    @pl.when(i < nu)
    def _():
        y_ref[...] = _dot(h_ref[...], wdb_ref[...])

    @pl.when(i >= nu)
    def _():
        y_ref[...] = jnp.zeros_like(y_ref)


def expert_down(h, blk_e, n_used, nxt_e, w_down, layer):
    S, Fd = h.shape
    _, E, _, D = w_down.shape
    tm = MOE_TM
    nb = S // tm
    kern = functools.partial(_expert_down_kernel, layer=layer)
    return pl.pallas_call(
        kern,
        grid_spec=pltpu.PrefetchScalarGridSpec(
            num_scalar_prefetch=3,
            grid=(nb,),
            in_specs=[
                pl.BlockSpec((tm, Fd), lambda i, be, nu, nx: (jnp.minimum(i, nu[0] - 1), 0)),
                pl.BlockSpec(memory_space=pl.ANY),
            ],
            out_specs=pl.BlockSpec((tm, D), lambda i, be, nu, nx: (i, 0)),
            scratch_shapes=[pltpu.VMEM((2, Fd, D), F32), pltpu.VMEM((Fd, D), BF16), pltpu.SMEM((1,), I32),
                            pltpu.SemaphoreType.DMA((2,))],
        ),
        out_shape=jax.ShapeDtypeStruct((S, D), F32),
        compiler_params=_cparams(("arbitrary",), 56),
        name="expert_down",
    )(blk_e, n_used, nxt_e, h, w_down)


def _combine_kernel(dcur_ref, dnext_ref, y_ref, w_ref, sh_ref, x1_ref, g_ref, mod_ref, o_ref, buf_ref, sem, *, tb, nb):
    i = pl.program_id(0)
    cur = lax.rem(i, 2)

    def fetch(dref, half):
        def issue(r, c):
            for k in range(TOP_K):
                d = dref[0, k * tb + r]
                pltpu.make_async_copy(y_ref.at[pl.ds(d, 1)], buf_ref.at[half, k, pl.ds(r, 1)], sem.at[half]).start()
            return c

        lax.fori_loop(0, tb, issue, 0, unroll=2)

    @pl.when(i == 0)
    def _():
        fetch(dcur_ref, 0)

    @pl.when(i + 1 < nb)
    def _():
        fetch(dnext_ref, 1 - cur)

    def drain(r, c):
        for k in range(TOP_K):
            pltpu.make_async_copy(y_ref.at[pl.ds(0, 1)], buf_ref.at[cur, k, pl.ds(0, 1)], sem.at[cur]).wait()
        return c

    lax.fori_loop(0, tb, drain, 0, unroll=2)
    w = w_ref[...]
    acc = sh_ref[...]
    for k in range(TOP_K):
        acc = acc + w[:, k:k + 1] * buf_ref[cur, k]
    o_ref[...] = x1_ref[...] + mod_ref[5:6, :] * (_rms(acc) * g_ref[3:4, :])


def combine(y_sorted, dest, w, shared, x1, g, mods, n_ctx, t_lat):
    N, D = x1.shape
    tb = 64
    nb = N // tb
    dest3 = _block_major(dest, tb)
    kern = functools.partial(_combine_kernel, tb=tb, nb=nb)
    return pl.pallas_call(
        kern,
        grid=(nb,),
        in_specs=[
            pl.BlockSpec((None, 1, TOP_K * tb), lambda i: (i, 0, 0), memory_space=pltpu.SMEM),
            pl.BlockSpec((None, 1, TOP_K * tb), lambda i: (jnp.minimum(i + 1, nb - 1), 0, 0), memory_space=pltpu.SMEM),
            pl.BlockSpec(memory_space=pl.ANY),
            pl.BlockSpec((tb, TOP_K), lambda i: (i, 0)),
            pl.BlockSpec((tb, D), lambda i: (i, 0)),
            pl.BlockSpec((tb, D), lambda i: (i, 0)),
            pl.BlockSpec((4, D), lambda i: (0, 0)),
            pl.BlockSpec((None, 6, D), lambda i: (_mod_row_index(i, tb, n_ctx, t_lat), 0, 0)),
        ],
        out_specs=pl.BlockSpec((tb, D), lambda i: (i, 0)),
        out_shape=jax.ShapeDtypeStruct((N, D), F32),
        scratch_shapes=[pltpu.VMEM((2, TOP_K, tb, D), F32), pltpu.SemaphoreType.DMA((2,))],
        compiler_params=_cparams(("arbitrary",), 40),
        name="combine",
    )(dest3, dest3, y_sorted, w, shared, x1, g, mods)


def moe_ffn(h2, hp, x1, p, g, mods, n_ctx, t_lat, layer):
    N, D = x1.shape
    tm = MOE_TM
    idx, w, rank, cnt = router(h2, p['router_wt'], p['router_b'])
    counts = cnt[:, 0].astype(I32)
    padded = (counts + tm - 1) // tm * tm
    pend = jnp.cumsum(padded)
    pstart = pend - padded
    n_blocks = (N * TOP_K + N_EXPERTS * (tm - 1) + tm - 1) // tm
    n_used = (pend[-1] // tm).astype(I32).reshape(1)
    blk_start = jnp.arange(n_blocks, dtype=I32) * tm
    blk_e = jnp.minimum(jnp.sum((pend[None, :] <= blk_start[:, None]).astype(I32), axis=1), N_EXPERTS - 1)
    hit = idx[:, :, None] == jnp.arange(N_EXPERTS, dtype=I32)[None, None, :]
    dest = jnp.sum(jnp.where(hit, pstart[None, None, :], 0), axis=-1) + rank
    after = jnp.sum(jnp.where(blk_e[:, None] == jnp.arange(N_EXPERTS, dtype=I32)[None, :], pend[None, :], 0),
                    axis=1) // tm
    nxt_e = jnp.where(after < n_used[0], blk_e[jnp.minimum(after, n_blocks - 1)], -1).astype(I32)
    n_fill = n_blocks * tm - N * TOP_K
    pad = padded - counts
    pad_end = jnp.cumsum(pad)
    q = jnp.arange(n_fill, dtype=I32)
    owner = jnp.sum((pad_end[None, :] <= q[:, None]).astype(I32), axis=1)
    oh = owner[:, None] == jnp.arange(N_EXPERTS, dtype=I32)[None, :]
    base = jnp.sum(jnp.where(oh, (pstart + counts - (pad_end - pad))[None, :], 0), axis=1)
    fill = jnp.where(owner < N_EXPERTS, base + q, pend[-1] + (q - pad_end[-1])).astype(I32)
    xs = dispatch(hp, dest, fill)
    hid = expert_up(xs, blk_e, n_used, nxt_e, p['exp_w_gate'], p['exp_w_up'], layer)
    ys = expert_down(hid, blk_e, n_used, nxt_e, p['exp_w_down'], layer)
    sh_hid = swiglu(h2, p['sh_w_gate'], p['sh_w_up'])
    shared = matmul(sh_hid, p['sh_w_down'])
    return combine(ys, dest, w.T, shared, x1, g, mods, n_ctx, t_lat)


def _axial_rope_tables(n):
    rows = n // GRID_W
    t = jnp.arange(rows * GRID_W)
    row = (t // GRID_W).astype(F32)
    col = (t % GRID_W).astype(F32)
    n_freq = LANES // 4
    inv = ROPE_BASE ** (-jnp.arange(n_freq, dtype=F32) / n_freq)
    ang = jnp.concatenate([row[:, None] * inv, col[:, None] * inv], axis=-1)
    cos, sin = jnp.cos(ang), jnp.sin(ang)
    return jnp.concatenate([cos, cos], axis=-1), jnp.concatenate([-sin, sin], axis=-1)


def _layer_params(l, w_in, conv_w, conv_b, lru_wa, lru_ba, lru_wx, lru_bx, lru_lam, diff_lam, diff_subln,
                  gla_w, gla_b, gla_norm, w_out, router_w, router_b, exp_w_gate, exp_w_up, exp_w_down,
                  sh_w_gate, sh_w_up, sh_w_down):
    d_lru = conv_w.shape[-1]
    dk = LANES
    wa, wx = lru_wa[l], lru_wx[l]
    w_gate = jnp.concatenate([wa[0], wx[0], wa[1], wx[1]], axis=-1).astype(BF16)
    ba = lru_ba[l].reshape(2, H_LRU, 1, -1)
    bx = lru_bx[l].reshape(2, H_LRU, 1, -1)
    b_gate = jnp.concatenate([ba[0], bx[0], ba[1], bx[1]], axis=-1)
    gw = gla_w[l]
    w_glog = jnp.zeros((H_GLA, LANES, 2 * dk), F32)
    for d in range(2):
        blk = gw[d].reshape(GLA_RANK, H_GLA, dk).transpose(1, 0, 2)
        w_glog = w_glog.at[:, d * GLA_RANK:(d + 1) * GLA_RANK, d * dk:(d + 1) * dk].set(blk)
    gb = gla_b[l].reshape(2, H_GLA, 1, dk)
    b_glog = jnp.concatenate([gb[0], gb[1]], axis=-1)
    n_pad = -w_in.shape[-1] % 512
    return {
        'w_main': jnp.pad(w_in[l], ((0, 0), (0, n_pad))).astype(BF16),
        'conv_w': conv_w[l], 'conv_b': conv_b[l].reshape(1, d_lru),
        'w_gate': w_gate, 'b_gate': b_gate, 'lru_lam': lru_lam[l],
        'diff_lam': diff_lam[l], 'diff_subln': diff_subln[l].reshape(1, -1),
        'w_glog': w_glog.astype(BF16), 'b_glog': b_glog, 'gla_norm': gla_norm[l].reshape(1, -1),
        'w_out': w_out[l].astype(BF16),
        'router_wt': router_w[l].T.astype(BF16),
        'router_b': jnp.broadcast_to(router_b[l][:, None], (N_EXPERTS, LANES)),
        'exp_w_gate': exp_w_gate, 'exp_w_up': exp_w_up, 'exp_w_down': exp_w_down,
        'sh_w_gate': sh_w_gate[l].astype(BF16), 'sh_w_up': sh_w_up[l].astype(BF16),
        'sh_w_down': sh_w_down[l].astype(BF16),
    }


def kernel(x_prompt, x_sample, cache_k, cache_v, state_lru, state_gla, c, c_ctx, mod_w, mod_b, norm_g, w_in,
           conv_w, conv_b, lru_wa, lru_ba, lru_wx, lru_bx, lru_lam, diff_lam, diff_subln, gla_w, gla_b,
           gla_norm, w_out, router_w, router_b, exp_w_gate, exp_w_up, exp_w_down, sh_w_gate, sh_w_up,
           sh_w_down):
    Bc, Tc, D = x_prompt.shape
    Bl, Tl, _ = x_sample.shape
    L = mod_w.shape[0]
    n_ctx, n_lat = Bc * Tc, Bl * Tl
    d_lru = conv_w.shape[-1]
    d_diff = H_DIFF * 2 * LANES
    d_gk = H_GLA * LANES
    d_gv = H_GLA * 2 * LANES
    col_qb = 2 * d_lru
    col_qc = col_qb + 3 * d_diff
    col_kc = col_qc + d_gk
    col_vc = col_kc + d_gk
    col_gc = col_vc + d_gv
    col_lr = col_gc + d_gv

    x = jnp.concatenate([x_prompt.reshape(n_ctx, D), x_sample.reshape(n_lat, D)], axis=0)
    n_rows = 16 * ((1 + Bl + 15) // 16)
    c_rows = jnp.zeros((n_rows, D), F32).at[0].set(c_ctx).at[1:1 + Bl].set(c)
    mods_all = modulation(c_rows, mod_w, mod_b).reshape(L, n_rows, 6, D)
    rope = _axial_rope_tables(Tl)
    cache_k2 = cache_k.reshape(*cache_k.shape[:3], -1)
    cache_v2 = cache_v.reshape(*cache_v.shape[:3], -1)
    zeros_lru = jnp.zeros((Bc, 2, d_lru), F32)
    zeros_gla = jnp.zeros((Bc, 2, H_GLA, LANES, 2 * LANES), F32)

    ks, vs, lrus, glas = [], [], [], []
    for l in range(L):
        p = _layer_params(l, w_in, conv_w, conv_b, lru_wa, lru_ba, lru_wx, lru_bx, lru_lam, diff_lam,
                          diff_subln, gla_w, gla_b, gla_norm, w_out, router_w, router_b, exp_w_gate,
                          exp_w_up, exp_w_down, sh_w_gate, sh_w_up, sh_w_down)
        lam_init = 0.8 - 0.6 * math.exp(-0.3 * l)
        mods = mods_all[l]
        g = norm_g[l]
        h = norm_mod(x, g[0:1], mods, n_ctx, Tl)
        z = matmul(h, p['w_main'])
        ycat = jnp.zeros((n_ctx + n_lat, D), BF16)
        ycat, lru_c = lru_mixer(z, ycat, 0, Bc, Tc, p['conv_w'], p['conv_b'], p['w_gate'], p['b_gate'],
                                p['lru_lam'], zeros_lru)
        ycat, _ = lru_mixer(z, ycat, n_ctx, Bl, Tl, p['conv_w'], p['conv_b'], p['w_gate'], p['b_gate'],
                            p['lru_lam'], state_lru[:, l])
        ycat = diff_attention(z, ycat, d_lru, 0, Bc, Tc, col_qb, None, None, l, None, p['diff_lam'],
                              p['diff_subln'], lam_init)
        ycat = diff_attention(z, ycat, d_lru, n_ctx, Bl, Tl, col_qb, cache_k2, cache_v2, l, rope, p['diff_lam'],
                              p['diff_subln'], lam_init)
        ycat, gla_c = gla_mixer(z, col_lr, ycat, d_lru + d_diff, 0, Bc, Tc, col_qc, col_kc, col_vc, col_gc,
                                p['w_glog'], p['b_glog'], p['gla_norm'], zeros_gla)
        ycat, _ = gla_mixer(z, col_lr, ycat, d_lru + d_diff, n_ctx, Bl, Tl, col_qc, col_kc, col_vc, col_gc,
                            p['w_glog'], p['b_glog'], p['gla_norm'], state_gla[:, l])
        y = matmul(ycat, p['w_out'])
        x1, h2, hp = post_mix(x, y, g, mods, n_ctx, Tl)
        x = moe_ffn(h2, hp, x1, p, g, mods, n_ctx, Tl, l)
        ks.append(z[:n_ctx, col_qb + d_diff:col_qb + 2 * d_diff].reshape(Bc, Tc, H_DIFF, 2 * LANES))
        vs.append(z[:n_ctx, col_qb + 2 * d_diff:col_qb + 3 * d_diff].reshape(Bc, Tc, H_DIFF, 2 * LANES))
        lrus.append(lru_c)
        glas.append(gla_c)
    yp = x[:n_ctx].reshape(Bc, Tc, D)
    ys = x[n_ctx:].reshape(Bl, Tl, D)
    return (yp, ys, jnp.stack(ks, axis=1), jnp.stack(vs, axis=1), jnp.stack(lrus, axis=1), jnp.stack(glas, axis=1))
```
